```python
import math
import jax
import jax.numpy as jnp
from jax import lax
import numpy as np

D_MODEL = 2048
BATCH = 4
SEQ = 2048
DEPTH = 2
DEC_BATCH = 8
DEC_SEQ = 32
PAST_LEN = 2048

CHUNK = 64
MIX_WIDTH = D_MODEL // 2
SB_HEADS = 8
SB_HEAD_DIM = MIX_WIDTH // SB_HEADS
SB_BLOCK = 128
LRU_WIDTH = MIX_WIDTH
LRU_BLOCKS = 8
LRU_BLOCK_DIM = LRU_WIDTH // LRU_BLOCKS
LRU_CONV = 4
LRU_C = 8.0
SSD_INNER = MIX_WIDTH
SSD_HEAD_DIM = 64
SSD_HEADS = SSD_INNER // SSD_HEAD_DIM
SSD_GROUPS = 2
SSD_STATE = 128
SSD_CONV = 4
SSD_CONV_DIM = SSD_INNER + 2 * SSD_GROUPS * SSD_STATE
N_BRANCH = 3
N_MEM = 256
XA_HEADS = 4
XA_HEAD_DIM = D_MODEL // XA_HEADS
D_FF = 5632
FFN_CONV = 3
NORM_EPS = 1e-6

IN_SIZES = (MIX_WIDTH, MIX_WIDTH, MIX_WIDTH, LRU_WIDTH, LRU_WIDTH, SSD_INNER, SSD_CONV_DIM, SSD_HEADS, N_BRANCH * D_MODEL)
IN_COLS = sum(IN_SIZES)
IN_SPLITS = tuple(int(s) for s in np.cumsum(IN_SIZES)[:-1])

kernel_name = 'hybrid_stickbreak_rglru_ssd_streaming_step'


def rmsnorm(x, g):
    xf = x.astype(jnp.float32)
    y = xf * lax.rsqrt(jnp.mean(xf * xf, axis=-1, keepdims=True) + NORM_EPS)
    return (y * g.astype(jnp.float32)).astype(x.dtype)


def causal_dwconv(x, buf, w, b):
    K = w.shape[0]
    L = x.shape[1]
    xp = jnp.concatenate([buf.astype(x.dtype), x], axis=1)
    y = b
    for k in range(K):
        y = y + w[k] * xp[:, k:k + L]
    return y, xp[:, L:]


def sb_attend_block(qb, qpos, k, v, kpos):
    z = jnp.einsum('bqhd,bshd->bhqs', qb, k).astype(jnp.float32) * (SB_HEAD_DIM ** -0.5)
    valid = kpos[None, :] < qpos[:, None]
    lneg = jnp.where(valid, jax.nn.log_sigmoid(-z), 0.0)
    after = lax.cumsum(lneg, axis=3, reverse=True) - lneg
    w = jnp.where(valid, jnp.exp(jax.nn.log_sigmoid(z) + after), 0.0)
    return jnp.einsum('bhqs,bshd->bqhd', w.astype(v.dtype), v)


def stick_breaking(q, k, v, q_offset):
    B, L, H, D = q.shape
    kpos = jnp.arange(k.shape[1])
    qpos = q_offset + jnp.arange(L)
    if L <= SB_BLOCK or L % SB_BLOCK:
        return sb_attend_block(q, qpos, k, v, kpos)
    nb = L // SB_BLOCK
    qb = jnp.moveaxis(q.reshape(B, nb, SB_BLOCK, H, D), 1, 0)
    pb = qpos.reshape(nb, SB_BLOCK)
    out = lax.map(lambda a: sb_attend_block(a[0], a[1], k, v, kpos), (qb, pb))
    return jnp.moveaxis(out, 0, 1).reshape(B, L, H, D)


def rg_lru(xc, h0, wa, ba, wx, bx, lam):
    B, L, W = xc.shape
    f32 = jnp.float32
    xf = xc.astype(f32)
    xblk = xf.reshape(B, L, LRU_BLOCKS, LRU_BLOCK_DIM)
    r = jax.nn.sigmoid(jnp.einsum('blnk,nkj->blnj', xblk, wa.astype(f32)).reshape(B, L, W) + ba.astype(f32))
    i = jax.nn.sigmoid(jnp.einsum('blnk,nkj->blnj', xblk, wx.astype(f32)).reshape(B, L, W) + bx.astype(f32))
    log_a = LRU_C * r * jax.nn.log_sigmoid(lam.astype(f32))
    a = jnp.exp(log_a)
    u = jnp.sqrt(-jnp.expm1(2.0 * log_a)) * (i * xf)
    u = u.at[:, 0].add(a[:, 0] * h0.astype(f32))

    def comb(e1, e2):
        a1, b1 = e1
        a2, b2 = e2
        return a1 * a2, a2 * b1 + b2

    _, h = lax.associative_scan(comb, (a, u), axis=1)
    return h.astype(xc.dtype), h[:, -1]


def ssd_chunked(x, dt, a, bm, cm, h0):
    b, L, H, P = x.shape
    G, N = bm.shape[2], bm.shape[3]
    E = H // G
    Q = math.gcd(L, CHUNK)
    nc = L // Q
    f32 = jnp.float32
    xc = x.astype(f32).reshape(b, nc, Q, G, E, P)
    dtc = dt.astype(f32).reshape(b, nc, Q, G, E)
    bc = bm.astype(f32).reshape(b, nc, Q, G, N)
    cc = cm.astype(f32).reshape(b, nc, Q, G, N)
    cum = jnp.cumsum(dtc * a.reshape(G, E), axis=2)
    seg = cum[:, :, :, None] - cum[:, :, None, :]
    causal = jnp.tril(jnp.ones((Q, Q), dtype=bool))[:, :, None, None]
    decay = jnp.exp(jnp.where(causal, seg, -jnp.inf))
    cb = jnp.einsum('bctgn,bcsgn->bctsg', cc, bc)
    y_intra = jnp.einsum('bctsge,bcsgep->bctgep', cb[..., None] * decay * dtc[:, :, None], xc)
    decay_end = jnp.exp(cum[:, :, -1:] - cum)
    s_chunk = jnp.einsum('bcsgn,bcsgep->bcgepn', bc, (decay_end * dtc)[..., None] * xc)
    chunk_decay = jnp.exp(cum[:, :, -1])

    def step(h, inp):
        s_c, d_c = inp
        return h * d_c[..., None, None] + s_c, h

    h_last, h_in = lax.scan(step, h0.astype(f32).reshape(b, G, E, P, N),
                            (jnp.moveaxis(s_chunk, 1, 0), jnp.moveaxis(chunk_decay, 1, 0)))
    h_in = jnp.moveaxis(h_in, 0, 1)
    y_inter = jnp.einsum('bctgn,bcgepn->bctgep', cc, h_in) * jnp.exp(cum)[..., None]
    y = (y_intra + y_inter).reshape(b, L, H, P)
    return y.astype(x.dtype), h_last.reshape(b, H, P, N)


def hybrid_mixer(xn, past_k, past_v, lru_buf, lru_h0, ssd_buf, ssd_h0, lp):
    B, L, _ = xn.shape
    proj = xn @ lp['w_in']
    q, k, v, lx, lg, z, xbc, dt, gate_logits = jnp.split(proj, IN_SPLITS, axis=-1)
    q = q.reshape(B, L, SB_HEADS, SB_HEAD_DIM)
    k = k.reshape(B, L, SB_HEADS, SB_HEAD_DIM)
    v = v.reshape(B, L, SB_HEADS, SB_HEAD_DIM)
    P = past_k.shape[1]
    k_all = jnp.concatenate([past_k.astype(k.dtype), k], axis=1)
    v_all = jnp.concatenate([past_v.astype(v.dtype), v], axis=1)
    o_a = stick_breaking(q, k_all, v_all, P).reshape(B, L, MIX_WIDTH)
    lx_c, lru_buf_new = causal_dwconv(lx, lru_buf, lp['lru_conv_w'], lp['lru_conv_b'])
    h, lru_h = rg_lru(lx_c, lru_h0, lp['lru_wa'], lp['lru_ba'], lp['lru_wx'], lp['lru_bx'], lp['lru_lam'])
    o_b = h * jax.nn.gelu(lg)
    xbc_c, ssd_buf_new = causal_dwconv(xbc, ssd_buf, lp['ssd_conv_w'], lp['ssd_conv_b'])
    xbc_c = jax.nn.silu(xbc_c)
    xs, bm, cm = jnp.split(xbc_c, (SSD_INNER, SSD_INNER + SSD_GROUPS * SSD_STATE), axis=-1)
    dts = jax.nn.softplus(dt.astype(jnp.float32) + lp['ssd_dt_bias'].astype(jnp.float32))
    a = -jnp.exp(lp['ssd_a_log'].astype(jnp.float32))
    xh = xs.reshape(B, L, SSD_HEADS, SSD_HEAD_DIM)
    y, ssd_h = ssd_chunked(xh, dts, a, bm.reshape(B, L, SSD_GROUPS, SSD_STATE),
                           cm.reshape(B, L, SSD_GROUPS, SSD_STATE), ssd_h0)
    y = y + lp['ssd_d'][:, None] * xh
    yg = (y.reshape(B, L, SSD_INNER) * jax.nn.silu(z)).reshape(B, L, SSD_GROUPS, SSD_INNER // SSD_GROUPS)
    o_c = rmsnorm(yg, lp['ssd_norm'].reshape(SSD_GROUPS, SSD_INNER // SSD_GROUPS)).reshape(B, L, SSD_INNER)
    branches = jnp.stack([o_a, o_b, o_c], axis=2)
    gates = jax.nn.sigmoid(gate_logits.reshape(B, L, N_BRANCH, D_MODEL))
    merged = jnp.sum(jnp.einsum('blnk,nkd->blnd', branches, lp['w_branch']) * gates, axis=2)
    return merged @ lp['w_out'], (k, v, lru_buf_new, lru_h, ssd_buf_new, ssd_h)


def mem_kv(mem, g, wk, wv):
    B, M, _ = mem.shape
    mn = rmsnorm(mem, g)
    return ((mn @ wk).reshape(B, M, XA_HEADS, XA_HEAD_DIM), (mn @ wv).reshape(B, M, XA_HEADS, XA_HEAD_DIM))


def cross_attn(xn, mk, mv, wq, wo):
    B, L, _ = xn.shape
    q = (xn @ wq).reshape(B, L, XA_HEADS, XA_HEAD_DIM)
    s = jnp.einsum('blhd,bmhd->bhlm', q, mk.astype(q.dtype)).astype(jnp.float32) * (XA_HEAD_DIM ** -0.5)
    p = jax.nn.softmax(s, axis=-1).astype(q.dtype)
    o = jnp.einsum('bhlm,bmhd->blhd', p, mv.astype(q.dtype)).reshape(B, L, D_MODEL)
    return o @ wo


def conv_ffn(xn, buf, w_up, cw, cb, w_down):
    u, buf_new = causal_dwconv(xn @ w_up, buf, cw, cb)
    g, v = jnp.split(u, 2, axis=-1)
    return (jax.nn.silu(g) * v) @ w_down, buf_new


def trunk_layer(x, mk, mv, past_k, past_v, lru_buf, lru_h, ssd_buf, ssd_h, ffn_buf, lp):
    m, st = hybrid_mixer(rmsnorm(x, lp['norm_mix']), past_k, past_v, lru_buf, lru_h, ssd_buf, ssd_h, lp)
    h = x + m
    h = h + cross_attn(rmsnorm(h, lp['norm_xattn']), mk, mv, lp['w_xq'], lp['w_xo'])
    f, ffn_new = conv_ffn(rmsnorm(h, lp['norm_ffn']), ffn_buf, lp['w_up'], lp['ffn_conv_w'], lp['ffn_conv_b'], lp['w_down'])
    return h + f, st + (ffn_new,)


def setup_inputs(seed: int = 0) -> dict:
    key = jax.random.key(seed)
    ks = iter(jax.random.split(key, 64))

    def nrm(shape, scale):
        return jax.random.normal(next(ks), shape, jnp.float32) * scale

    def gain(shape):
        return 1.0 + nrm(shape, 0.02)

    def unif(shape, lo, hi):
        return jax.random.uniform(next(ks), shape, jnp.float32, lo, hi)

    dt0 = jnp.exp(unif((DEPTH, SSD_HEADS), math.log(1e-3), math.log(1e-1)))
    a_base = unif((DEPTH, LRU_WIDTH), 0.9, 0.999) ** (1.0 / LRU_C)
    a_log = jnp.log(unif((DEPTH, SSD_HEADS), 1.0, 16.0))
    return {
        'x_prompt': nrm((BATCH, SEQ, D_MODEL), 1.0),
        'x_sample': nrm((DEC_BATCH, DEC_SEQ, D_MODEL), 1.0),
        'mem_prompt': nrm((BATCH, N_MEM, D_MODEL), 1.0),
        'cache_sb_k': nrm((DEPTH, DEC_BATCH, PAST_LEN, SB_HEADS, SB_HEAD_DIM), 1.0),
        'cache_sb_v': nrm((DEPTH, DEC_BATCH, PAST_LEN, SB_HEADS, SB_HEAD_DIM), 1.0),
        'cache_mem_k': nrm((DEPTH, DEC_BATCH, N_MEM, XA_HEADS, XA_HEAD_DIM), 1.0),
        'cache_mem_v': nrm((DEPTH, DEC_BATCH, N_MEM, XA_HEADS, XA_HEAD_DIM), 1.0),
        'state_lru_conv': nrm((DEPTH, DEC_BATCH, LRU_CONV - 1, LRU_WIDTH), 1.0),
        'state_lru_h': nrm((DEPTH, DEC_BATCH, LRU_WIDTH), 0.5),
        'state_ssd_conv': nrm((DEPTH, DEC_BATCH, SSD_CONV - 1, SSD_CONV_DIM), 1.0),
        'state_ssd': nrm((DEPTH, DEC_BATCH, SSD_HEADS, SSD_HEAD_DIM, SSD_STATE), 0.1),
        'state_ffn_conv': nrm((DEPTH, DEC_BATCH, FFN_CONV - 1, 2 * D_FF), 1.0),
        'norm_mix': gain((DEPTH, D_MODEL)),
        'w_in': nrm((DEPTH, D_MODEL, IN_COLS), D_MODEL ** -0.5),
        'lru_conv_w': nrm((DEPTH, LRU_CONV, LRU_WIDTH), LRU_CONV ** -0.5),
        'lru_conv_b': nrm((DEPTH, LRU_WIDTH), 0.01),
        'lru_wa': nrm((DEPTH, LRU_BLOCKS, LRU_BLOCK_DIM, LRU_BLOCK_DIM), LRU_BLOCK_DIM ** -0.5),
        'lru_ba': nrm((DEPTH, LRU_WIDTH), 0.01),
        'lru_wx': nrm((DEPTH, LRU_BLOCKS, LRU_BLOCK_DIM, LRU_BLOCK_DIM), LRU_BLOCK_DIM ** -0.5),
        'lru_bx': nrm((DEPTH, LRU_WIDTH), 0.01),
        'lru_lam': jnp.log(a_base) - jnp.log1p(-a_base),
        'ssd_conv_w': nrm((DEPTH, SSD_CONV, SSD_CONV_DIM), SSD_CONV ** -0.5),
        'ssd_conv_b': nrm((DEPTH, SSD_CONV_DIM), 0.01),
        'ssd_dt_bias': dt0 + jnp.log(-jnp.expm1(-dt0)),
        'ssd_a_log': a_log,
        'ssd_d': gain((DEPTH, SSD_HEADS)),
        'ssd_norm': gain((DEPTH, SSD_INNER)),
        'w_branch': nrm((DEPTH, N_BRANCH, MIX_WIDTH, D_MODEL), MIX_WIDTH ** -0.5),
        'w_out': nrm((DEPTH, D_MODEL, D_MODEL), D_MODEL ** -0.5),
        'norm_xattn': gain((DEPTH, D_MODEL)),
        'norm_mem': gain((DEPTH, D_MODEL)),
        'w_xq': nrm((DEPTH, D_MODEL, D_MODEL), D_MODEL ** -0.5),
        'w_xk': nrm((DEPTH, D_MODEL, D_MODEL), D_MODEL ** -0.5),
        'w_xv': nrm((DEPTH, D_MODEL, D_MODEL), D_MODEL ** -0.5),
        'w_xo': nrm((DEPTH, D_MODEL, D_MODEL), D_MODEL ** -0.5),
        'norm_ffn': gain((DEPTH, D_MODEL)),
        'w_up': nrm((DEPTH, D_MODEL, 2 * D_FF), D_MODEL ** -0.5),
        'ffn_conv_w': nrm((DEPTH, FFN_CONV, 2 * D_FF), FFN_CONV ** -0.5),
        'ffn_conv_b': nrm((DEPTH, 2 * D_FF), 0.01),
        'w_down': nrm((DEPTH, D_FF, D_MODEL), D_FF ** -0.5),
        'norm_final': gain((D_MODEL,)),
    }


def reference(x_prompt, x_sample, mem_prompt, cache_sb_k, cache_sb_v, cache_mem_k, cache_mem_v,
              state_lru_conv, state_lru_h, state_ssd_conv, state_ssd, state_ffn_conv,
              norm_mix, w_in, lru_conv_w, lru_conv_b, lru_wa, lru_ba, lru_wx, lru_bx, lru_lam,
              ssd_conv_w, ssd_conv_b, ssd_dt_bias, ssd_a_log, ssd_d, ssd_norm, w_branch, w_out,
              norm_xattn, norm_mem, w_xq, w_xk, w_xv, w_xo, norm_ffn, w_up, ffn_conv_w, ffn_conv_b,
              w_down, norm_final):
    bp = x_prompt.shape[0]
    dtp = x_prompt.dtype
    empty_kv = jnp.zeros((bp, 0, SB_HEADS, SB_HEAD_DIM), dtp)
    zero_lru_conv = jnp.zeros((bp, LRU_CONV - 1, LRU_WIDTH), dtp)
    zero_lru_h = jnp.zeros((bp, LRU_WIDTH), jnp.float32)
    zero_ssd_conv = jnp.zeros((bp, SSD_CONV - 1, SSD_CONV_DIM), dtp)
    zero_ssd = jnp.zeros((bp, SSD_HEADS, SSD_HEAD_DIM, SSD_STATE), jnp.float32)
    zero_ffn_conv = jnp.zeros((bp, FFN_CONV - 1, 2 * D_FF), dtp)

    yp, ys = x_prompt, x_sample
    st_p, st_s, mk_list, mv_list = [], [], [], []
    for l in range(DEPTH):
        lp = dict(norm_mix=norm_mix[l], w_in=w_in[l], lru_conv_w=lru_conv_w[l], lru_conv_b=lru_conv_b[l],
                  lru_wa=lru_wa[l], lru_ba=lru_ba[l], lru_wx=lru_wx[l], lru_bx=lru_bx[l], lru_lam=lru_lam[l],
                  ssd_conv_w=ssd_conv_w[l], ssd_conv_b=ssd_conv_b[l], ssd_dt_bias=ssd_dt_bias[l],
                  ssd_a_log=ssd_a_log[l], ssd_d=ssd_d[l], ssd_norm=ssd_norm[l], w_branch=w_branch[l],
                  w_out=w_out[l], norm_xattn=norm_xattn[l], w_xq=w_xq[l], w_xo=w_xo[l], norm_ffn=norm_ffn[l],
                  w_up=w_up[l], ffn_conv_w=ffn_conv_w[l], ffn_conv_b=ffn_conv_b[l], w_down=w_down[l])
        mk_p, mv_p = mem_kv(mem_prompt, norm_mem[l], w_xk[l], w_xv[l])
        yp, sp = trunk_layer(yp, mk_p, mv_p, empty_kv, empty_kv, zero_lru_conv, zero_lru_h,
                             zero_ssd_conv, zero_ssd, zero_ffn_conv, lp)
        ys, ss = trunk_layer(ys, cache_mem_k[l], cache_mem_v[l], cache_sb_k[l], cache_sb_v[l],
                             state_lru_conv[l], state_lru_h[l], state_ssd_conv[l], state_ssd[l],
                             state_ffn_conv[l], lp)
        st_p.append(sp)
        st_s.append(ss)
        mk_list.append(mk_p)
        mv_list.append(mv_p)
    y_prompt = rmsnorm(yp, norm_final)
    y_sample = rmsnorm(ys, norm_final)

    def stk(lst, i):
        return jnp.stack([s[i] for s in lst], axis=0)

    return (y_prompt, y_sample,
            stk(st_p, 0), stk(st_p, 1), stk(st_p, 2), stk(st_p, 3), stk(st_p, 4), stk(st_p, 5), stk(st_p, 6),
            jnp.stack(mk_list, axis=0), jnp.stack(mv_list, axis=0),
            stk(st_s, 0), stk(st_s, 1), stk(st_s, 2), stk(st_s, 3), stk(st_s, 4), stk(st_s, 5), stk(st_s, 6))
```

```python
import functools
import math

import jax
import jax.numpy as jnp
from jax import lax
from jax.experimental import pallas as pl
from jax.experimental.pallas import tpu as pltpu

F32 = jnp.float32
BF16 = jnp.bfloat16

NORM_EPS = 1e-6
LRU_C = 8.0
LANES = 128
SUBLANES = 8
VMEM_LIMIT_BYTES = 56 * 1024 * 1024

SB_HEADS = 8
SB_HEAD_DIM = 128
LRU_BLOCKS = 8
SSD_HEADS = 16
SSD_HEAD_DIM = 64
SSD_GROUPS = 2
SSD_STATE = 128
XA_HEADS = 4


def _params(*sem):
    return pltpu.CompilerParams(dimension_semantics=sem, vmem_limit_bytes=VMEM_LIMIT_BYTES)


def _dot(a, b):
    return jnp.dot(a, b, preferred_element_type=F32)


def _dot_nt(a, b):
    return lax.dot_general(a, b, (((1,), (1,)), ((), ())), preferred_element_type=F32)


def _split3(x):
    hi = x.astype(BF16)
    r = x - hi.astype(F32)
    mid = r.astype(BF16)
    lo = (r - mid.astype(F32)).astype(BF16)
    return hi, mid, lo


def _softplus(x):
    return jnp.maximum(x, 0.0) + jnp.log1p(jnp.exp(-jnp.abs(x)))


def _silu(x):
    return x * jax.nn.sigmoid(x)


def _gelu_tanh(x):
    c = math.sqrt(2.0 / math.pi)
    return 0.5 * x * (1.0 + jnp.tanh(c * (x + 0.044715 * (x * x * x))))


def _norm_matmul_kernel(x_ref, g_ref, w_ref, o_ref, xn_ref):
    @pl.when(pl.program_id(1) == 0)
    def _():
        x = x_ref[...]
        ms = jnp.mean(x * x, axis=-1, keepdims=True)
        xn_ref[...] = (x * lax.rsqrt(ms + NORM_EPS) * g_ref[...]).astype(BF16)

    o_ref[...] = _dot(xn_ref[...], w_ref[...]).astype(o_ref.dtype)


def norm_matmul(x, g, w, out_dtype=F32, tm=1024, tn=512):
    T, D = x.shape
    N = w.shape[1]
    tm = min(tm, T)
    tn = min(tn, N)
    return pl.pallas_call(
        _norm_matmul_kernel,
        grid=(T // tm, N // tn),
        in_specs=[
            pl.BlockSpec((tm, D), lambda i, j: (i, 0)),
            pl.BlockSpec((1, D), lambda i, j: (0, 0)),
            pl.BlockSpec((D, tn), lambda i, j: (0, j)),
        ],
        out_specs=pl.BlockSpec((tm, tn), lambda i, j: (i, j)),
        out_shape=jax.ShapeDtypeStruct((T, N), out_dtype),
        scratch_shapes=[pltpu.VMEM((tm, D), BF16)],
        compiler_params=_params("parallel", "arbitrary"),
        name="norm_matmul",
    )(x, g.reshape(1, D), w)


def _matmul_res_kernel(x_ref, w_ref, r_ref, o_ref):
    o_ref[...] = r_ref[...] + _dot(x_ref[...], w_ref[...])


def matmul_res(x, w, res, tm=1024, tn=512):
    T, K = x.shape
    N = w.shape[1]
    tm = min(tm, T)
    tn = min(tn, N)
    return pl.pallas_call(
        _matmul_res_kernel,
        grid=(T // tm, N // tn),
        in_specs=[
            pl.BlockSpec((tm, K), lambda i, j: (i, 0)),
            pl.BlockSpec((K, tn), lambda i, j: (0, j)),
            pl.BlockSpec((tm, tn), lambda i, j: (i, j)),
        ],
        out_specs=pl.BlockSpec((tm, tn), lambda i, j: (i, j)),
        out_shape=jax.ShapeDtypeStruct((T, N), F32),
        compiler_params=_params("parallel", "arbitrary"),
        name="matmul_res",
    )(x, w, res)


def _rmsnorm_kernel(x_ref, g_ref, o_ref):
    x = x_ref[...]
    ms = jnp.mean(x * x, axis=-1, keepdims=True)
    o_ref[...] = x * lax.rsqrt(ms + NORM_EPS) * g_ref[...]


def rmsnorm(x, g, tm=512):
    T, D = x.shape
    tm = min(tm, T)
    return pl.pallas_call(
        _rmsnorm_kernel,
        grid=(T // tm,),
        in_specs=[pl.BlockSpec((tm, D), lambda i: (i, 0)), pl.BlockSpec((1, D), lambda i: (0, 0))],
        out_specs=pl.BlockSpec((tm, D), lambda i: (i, 0)),
        out_shape=jax.ShapeDtypeStruct((T, D), F32),
        compiler_params=_params("parallel"),
        name="rmsnorm",
    )(x, g.reshape(1, D))


def _sb_kernel(q_ref, k_ref, v_ref, o_ref, *, tq, tk, q_offset, nk_total):
    qi = pl.program_id(2)
    q = q_ref[...].astype(BF16)
    qpos0 = q_offset + qi * tq
    nkb = jnp.minimum(nk_total, (qpos0 + tq - 2) // tk + 1)
    row = qpos0 + lax.broadcasted_iota(jnp.int32, (tq, tk), 0)
    col = lax.broadcasted_iota(jnp.int32, (tq, tk), 1)
    rj = lax.broadcasted_iota(jnp.int32, (tk, 2 * tk), 0)
    rs = lax.broadcasted_iota(jnp.int32, (tk, 2 * tk), 1)
    tri = jnp.where((rj >= rs) | (rs >= tk), 1.0, 0.0).astype(BF16)
    scale = SB_HEAD_DIM ** -0.5

    def body(it, carry_acc):
        carry, acc = carry_acc
        k0 = pl.multiple_of((nkb - 1 - it) * tk, tk)
        k = k_ref[pl.ds(k0, tk), :].astype(BF16)
        v = v_ref[pl.ds(k0, tk), :].astype(BF16)
        z = _dot_nt(q, k) * scale
        valid = (k0 + col) < row
        lneg = jnp.where(valid, -_softplus(z), 0.0)
        hi = lneg.astype(BF16)
        lo = (lneg - hi.astype(F32)).astype(BF16)
        cs = _dot(hi, tri) + _dot(lo, tri)
        w = jnp.where(valid, jnp.exp(z + carry + cs[:, :tk]), 0.0)
        acc = acc + _dot(w.astype(BF16), v)
        carry = carry + cs[:, tk:]
        return carry, acc

    init = (jnp.zeros((tq, tk), F32), jnp.zeros((tq, SB_HEAD_DIM), F32))
    _, acc = lax.fori_loop(0, nkb, body, init)
    o_ref[...] = acc.astype(o_ref.dtype)


def sb_attention(q, k_all, v_all, B, L, q_offset, tq=256, tk=128):
    S = k_all.shape[1]
    tq = min(tq, L)
    nq = L // tq
    kern = functools.partial(_sb_kernel, tq=tq, tk=tk, q_offset=q_offset, nk_total=S // tk)
    return pl.pallas_call(
        kern,
        grid=(B, SB_HEADS, nq),
        in_specs=[
            pl.BlockSpec((tq, SB_HEAD_DIM), lambda b, h, i: (b * nq + i, h)),
            pl.BlockSpec((None, S, SB_HEAD_DIM), lambda b, h, i: (b, 0, h)),
            pl.BlockSpec((None, S, SB_HEAD_DIM), lambda b, h, i: (b, 0, h)),
        ],
        out_specs=pl.BlockSpec((tq, SB_HEAD_DIM), lambda b, h, i: (b * nq + i, h)),
        out_shape=jax.ShapeDtypeStruct((B * L, SB_HEADS * SB_HEAD_DIM), BF16),
        compiler_params=_params("parallel", "parallel", "arbitrary"),
        name="sb_attention",
    )(q, k_all, v_all)


def _conv_taps(xp_ref, x3, halo, K, sl):
    xp_ref[:, SUBLANES - (K - 1):SUBLANES, :] = halo
    xp_ref[:, SUBLANES:SUBLANES + sl, :] = x3
    base = SUBLANES - (K - 1)
    return [xp_ref[:, base + k:base + k + sl, :] for k in range(K)]


def conv_halos(x, state, B, L, tile):
    C = x.shape[1]
    km1 = state.shape[1]
    nt = L // tile
    if nt == 1:
        return state[:, None]
    tails = x.reshape(B, nt, tile, C)[:, :-1, tile - km1:, :]
    return jnp.concatenate([state[:, None], tails], axis=1)


def _lru_kernel(lx_ref, lg_ref, halo_ref, h0_ref, cw_ref, cb_ref, wa_ref, ba_ref, wx_ref, bx_ref,
                lam_ref, o_ref, hl_ref, xp_ref, a_ref, u_ref, h_ref, hc_ref, *, tl, nl):
    i = pl.program_id(1)
    W = lx_ref.shape[1]

    @pl.when(i == 0)
    def _():
        hc_ref[...] = h0_ref[...]

    taps = _conv_taps(xp_ref, lx_ref[...][None], halo_ref[...][None], 4, tl)
    xc = cb_ref[...]
    for k in range(4):
        xc = xc + cw_ref[k:k + 1, :] * taps[k][0]

    bd = W // LRU_BLOCKS
    for n in range(LRU_BLOCKS):
        sl = slice(n * bd, (n + 1) * bd)
        xn = xc[:, sl]
        xb = xn.astype(BF16)
        r = jax.nn.sigmoid(_dot(xb, wa_ref[n]) + ba_ref[:, sl])
        ig = jax.nn.sigmoid(_dot(xb, wx_ref[n]) + bx_ref[:, sl])
        log_a = LRU_C * r * (-_softplus(-lam_ref[:, sl]))
        a = jnp.exp(log_a)
        u = jnp.sqrt(-jnp.tanh(log_a) * (a * a + 1.0)) * (ig * xn)
        a_ref[:, sl] = a
        u_ref[:, sl] = u

    def step(t, h):
        h = a_ref[pl.ds(t, 1), :] * h + u_ref[pl.ds(t, 1), :]
        h_ref[pl.ds(t, 1), :] = h
        return h

    h = lax.fori_loop(0, tl, step, hc_ref[...], unroll=8)
    hc_ref[...] = h
    o_ref[...] = (h_ref[...] * _gelu_tanh(lg_ref[...])).astype(o_ref.dtype)

    @pl.when(i == nl - 1)
    def _():
        hl_ref[...] = h


def lru_branch(lx, lg, state, h0, cw, cb, wa, ba, wx, bx, lam, B, L, tl=256):
    W = lx.shape[1]
    tl = min(tl, L)
    nl = L // tl
    halo = conv_halos(lx, state, B, L, tl)
    kern = functools.partial(_lru_kernel, tl=tl, nl=nl)
    vec = lambda: pl.BlockSpec((1, W), lambda b, i: (0, 0))
    blk = lambda: pl.BlockSpec(wa.shape, lambda b, i: (0, 0, 0))
    o, hl = pl.pallas_call(
        kern,
        grid=(B, nl),
        in_specs=[
            pl.BlockSpec((tl, W), lambda b, i: (b * nl + i, 0)),
            pl.BlockSpec((tl, W), lambda b, i: (b * nl + i, 0)),
            pl.BlockSpec((None, None, 3, W), lambda b, i: (b, i, 0, 0)),
            pl.BlockSpec((None, 1, W), lambda b, i: (b, 0, 0)),
            pl.BlockSpec((4, W), lambda b, i: (0, 0)),
            vec(), blk(), vec(), blk(), vec(), vec(),
        ],
        out_specs=[
            pl.BlockSpec((tl, W), lambda b, i: (b * nl + i, 0)),
            pl.BlockSpec((None, 1, W), lambda b, i: (b, 0, 0)),
        ],
        out_shape=[
            jax.ShapeDtypeStruct((B * L, W), BF16),
            jax.ShapeDtypeStruct((B, 1, W), F32),
        ],
        scratch_shapes=[
            pltpu.VMEM((1, tl + SUBLANES, W), F32),
            pltpu.VMEM((tl, W), F32),
            pltpu.VMEM((tl, W), F32),
            pltpu.VMEM((tl, W), F32),
            pltpu.VMEM((1, W), F32),
        ],
        compiler_params=_params("parallel", "arbitrary"),
        name="lru_branch",
    )(lx, lg, halo, h0.reshape(B, 1, W), cw, cb.reshape(1, W), wa.astype(BF16), ba.reshape(1, W),
      wx.astype(BF16), bx.reshape(1, W), lam.reshape(1, W))
    return o, hl.reshape(B, W)


def _ssd_kernel(z_ref, xbc_ref, dt_ref, halo_ref, h0_ref, cw_ref, cb_ref, dtb_ref, alog_ref,
                dexp_ref, nw_ref, o_ref, hl_ref, xp_ref, y_ref, st_ref, *, q, nc, n_valid):
    c = pl.program_id(1)
    inner = SSD_HEADS * SSD_HEAD_DIM
    gw = inner // SSD_GROUPS
    epg = SSD_HEADS // SSD_GROUPS
    n = SSD_STATE

    @pl.when(c == 0)
    def _():
        st_ref[...] = h0_ref[...]

    taps = _conv_taps(xp_ref, xbc_ref[...][None], halo_ref[...][None], 4, q)
    xc = cb_ref[...]
    for k in range(4):
        xc = xc + cw_ref[k:k + 1, :] * taps[k][0]
    xc = _silu(xc)
    xs = xc[:, :inner]
    bm = xc[:, inner:inner + SSD_GROUPS * n]
    cm = xc[:, inner + SSD_GROUPS * n:]

    rowi = lax.broadcasted_iota(jnp.int32, (q, LANES), 0)
    dts = _softplus(dt_ref[...] + dtb_ref[...])
    if n_valid < q:
        dts = jnp.where(rowi < n_valid, dts, 0.0)
    dta = dts * (-jnp.exp(alog_ref[...]))

    ti = lax.broadcasted_iota(jnp.int32, (q, q), 0)
    si = lax.broadcasted_iota(jnp.int32, (q, q), 1)
    causal = si <= ti
    lower = jnp.where(causal, 1.0, 0.0).astype(BF16)
    cum = sum(_dot(lower, p) for p in _split3(dta))
    cum_t = cum.T
    cum_last = cum[q - 1:q, :]
    exp_cum = jnp.exp(cum)
    decay_end = jnp.exp(cum_last - cum)

    eh = lax.broadcasted_iota(jnp.int32, (LANES, inner), 0)
    ec = lax.broadcasted_iota(jnp.int32, (LANES, inner), 1)
    expand = jnp.where(ec // SSD_HEAD_DIM == eh, 1.0, 0.0).astype(BF16)

    def expand_heads(a):
        return sum(_dot(p, expand) for p in _split3(a))

    dts_e = expand_heads(dts)
    exp_cum_e = expand_heads(exp_cum)
    decay_end_e = expand_heads(decay_end)
    chunk_decay_e = exp_cum_e[q - 1:q, :]

    xdt = xs * dts_e
    xdt_b = xdt.astype(BF16)
    xend_b = (decay_end_e * xdt).astype(BF16)

    for g in range(SSD_GROUPS):
        bg = bm[:, g * n:(g + 1) * n]
        cg_b = cm[:, g * n:(g + 1) * n].astype(BF16)
        gs = slice(g * gw, (g + 1) * gw)
        cb = _dot_nt(cg_b, bg.astype(BF16))
        st = st_ref[g]
        y_inter = _dot(cg_b, st.astype(BF16)) * exp_cum_e[:, gs]
        for e in range(epg):
            h = g * epg + e
            hs = slice(h * SSD_HEAD_DIM, (h + 1) * SSD_HEAD_DIM)
            seg = cum[:, h:h + 1] - cum_t[h:h + 1, :]
            decay = jnp.where(causal, jnp.exp(seg), 0.0)
            wts = (cb * decay).astype(BF16)
            y_ref[:, hs] = _dot(wts, xdt_b[:, hs]) + y_inter[:, e * SSD_HEAD_DIM:(e + 1) * SSD_HEAD_DIM]
        bg_t = bg.T.astype(BF16)
        st_ref[g] = chunk_decay_e[:, gs] * st + _dot(bg_t, xend_b[:, gs])

    y = y_ref[...] + dexp_ref[...] * xs
    yg = y * _silu(z_ref[...])
    for g in range(SSD_GROUPS):
        gs = slice(g * gw, (g + 1) * gw)
        part = yg[:, gs]
        ms = jnp.mean(part * part, axis=-1, keepdims=True)
        o_ref[:, gs] = (part * lax.rsqrt(ms + NORM_EPS) * nw_ref[:, gs]).astype(o_ref.dtype)

    @pl.when(c == nc - 1)
    def _():
        hl_ref[...] = st_ref[...]


def _pad_rows(x, B, L, Lp):
    if Lp == L:
        return x
    C = x.shape[1]
    return jnp.pad(x.reshape(B, L, C), ((0, 0), (0, Lp - L), (0, 0))).reshape(B * Lp, C)


def ssd_branch(z, xbc, dt, state, h0, cw, cb, dt_bias, a_log, d, norm_w, B, L, q=128):
    inner = z.shape[1]
    cd = xbc.shape[1]
    nh = SSD_HEADS
    Lp = max(L, q) if L % q else L
    n_valid = min(L, q)
    nc = Lp // q
    gw = inner // SSD_GROUPS
    epg = SSD_HEADS // SSD_GROUPS
    halo = conv_halos(xbc, state, B, L, q) if L >= q else state[:, None]
    zp = _pad_rows(z, B, L, Lp)
    xbcp = _pad_rows(xbc, B, L, Lp)
    dtp = _pad_rows(dt, B, L, Lp)
    h0t = h0.reshape(B, SSD_GROUPS, epg, SSD_HEAD_DIM, SSD_STATE).transpose(0, 1, 4, 2, 3)
    h0t = h0t.reshape(B, SSD_GROUPS, SSD_STATE, gw)
    pad1 = lambda a: jnp.pad(a, (0, LANES - nh)).reshape(1, LANES)
    kern = functools.partial(_ssd_kernel, q=q, nc=nc, n_valid=n_valid)
    row = lambda w: pl.BlockSpec((q, w), lambda b, c: (b * nc + c, 0))
    vec = lambda w: pl.BlockSpec((1, w), lambda b, c: (0, 0))
    o, hl = pl.pallas_call(
        kern,
        grid=(B, nc),
        in_specs=[
            row(inner), row(cd), row(LANES),
            pl.BlockSpec((None, None, 3, cd), lambda b, c: (b, c, 0, 0)),
            pl.BlockSpec((None, SSD_GROUPS, SSD_STATE, gw), lambda b, c: (b, 0, 0, 0)),
            pl.BlockSpec((4, cd), lambda b, c: (0, 0)),
            vec(cd), vec(LANES), vec(LANES), vec(inner), vec(inner),
        ],
        out_specs=[
            row(inner),
            pl.BlockSpec((None, SSD_GROUPS, SSD_STATE, gw), lambda b, c: (b, 0, 0, 0)),
        ],
        out_shape=[
            jax.ShapeDtypeStruct((B * Lp, inner), BF16),
            jax.ShapeDtypeStruct((B, SSD_GROUPS, SSD_STATE, gw), F32),
        ],
        scratch_shapes=[
            pltpu.VMEM((1, q + SUBLANES, cd), F32),
            pltpu.VMEM((q, inner), F32),
            pltpu.VMEM((SSD_GROUPS, SSD_STATE, gw), F32),
        ],
        compiler_params=_params("parallel", "arbitrary"),
        name="ssd_branch",
    )(zp, xbcp, dtp, halo, h0t, cw, cb.reshape(1, cd), pad1(dt_bias), pad1(a_log),
      jnp.repeat(d, SSD_HEAD_DIM).reshape(1, inner), norm_w.reshape(1, inner))
    if Lp != L:
        o = o.reshape(B, Lp, inner)[:, :L].reshape(B * L, inner)
    hl = hl.reshape(B, SSD_GROUPS, SSD_STATE, epg, SSD_HEAD_DIM).transpose(0, 1, 3, 4, 2)
    return o, hl.reshape(B, SSD_HEADS, SSD_HEAD_DIM, SSD_STATE)


def _merge_kernel(a_ref, b_ref, c_ref, g0_ref, g1_ref, g2_ref, wb_ref, o_ref):
    acc = None
    for n, (br, gr) in enumerate(((a_ref, g0_ref), (b_ref, g1_ref), (c_ref, g2_ref))):
        t = _dot(br[...], wb_ref[n]) * jax.nn.sigmoid(gr[...])
        acc = t if acc is None else acc + t
    o_ref[...] = acc.astype(o_ref.dtype)


def gated_merge(oa, ob, oc, gates, wb, tm=1024, tn=512):
    T, K = oa.shape
    N = wb.shape[2]
    tm = min(tm, T)
    nj = N // tn
    br = lambda: pl.BlockSpec((tm, K), lambda i, j: (i, 0))
    gate = lambda n: pl.BlockSpec((tm, tn), lambda i, j: (i, n * nj + j))
    return pl.pallas_call(
        _merge_kernel,
        grid=(T // tm, nj),
        in_specs=[br(), br(), br(), gate(0), gate(1), gate(2),
                  pl.BlockSpec((3, K, tn), lambda i, j: (0, 0, j))],
        out_specs=pl.BlockSpec((tm, tn), lambda i, j: (i, j)),
        out_shape=jax.ShapeDtypeStruct((T, N), BF16),
        compiler_params=_params("parallel", "arbitrary"),
        name="gated_merge",
    )(oa, ob, oc, gates, gates, gates, wb)


def _xattn_kernel(q_ref, mk_ref, mv_ref, o_ref):
    hd = q_ref.shape[1] // XA_HEADS
    scale = hd ** -0.5
    for h in range(XA_HEADS):
        sl = slice(h * hd, (h + 1) * hd)
        s = _dot_nt(q_ref[:, sl].astype(BF16), mk_ref[:, sl].astype(BF16)) * scale
        m = jnp.max(s, axis=-1, keepdims=True)
        p = jnp.exp(s - m)
        p = p / jnp.sum(p, axis=-1, keepdims=True)
        o_ref[:, sl] = _dot(p.astype(BF16), mv_ref[:, sl].astype(BF16)).astype(o_ref.dtype)


def cross_attention(q, mk, mv, B, L, tq=512):
    D = q.shape[1]
    M = mk.shape[1]
    tq = min(tq, L)
    nq = L // tq
    return pl.pallas_call(
        _xattn_kernel,
        grid=(B, nq),
        in_specs=[
            pl.BlockSpec((tq, D), lambda b, i: (b * nq + i, 0)),
            pl.BlockSpec((None, M, D), lambda b, i: (b, 0, 0)),
            pl.BlockSpec((None, M, D), lambda b, i: (b, 0, 0)),
        ],
        out_specs=pl.BlockSpec((tq, D), lambda b, i: (b * nq + i, 0)),
        out_shape=jax.ShapeDtypeStruct((B * L, D), BF16),
        compiler_params=_params("parallel", "arbitrary"),
        name="cross_attention",
    )(q, mk, mv)


def _ffn_down_kernel(ug_ref, uv_ref, hg_ref, hv_ref, cwg_ref, cwv_ref, cbg_ref, cbv_ref, wd_ref,
                     r_ref, o_ref, xp_ref, acc_ref, *, nseg, sl, nf):
    f = pl.program_id(1)
    tf = ug_ref.shape[1]

    def conv(u_ref, h_ref, cw_ref, cb_ref):
        taps = _conv_taps(xp_ref, u_ref[...].reshape(nseg, sl, tf), h_ref[...], 3, sl)
        y = cb_ref[...]
        for k in range(3):
            y = y + cw_ref[k:k + 1, :] * taps[k].reshape(nseg * sl, tf)
        return y

    g = conv(ug_ref, hg_ref, cwg_ref, cbg_ref)
    v = conv(uv_ref, hv_ref, cwv_ref, cbv_ref)
    part = _dot((_silu(g) * v).astype(BF16), wd_ref[...])

    @pl.when(f == 0)
    def _():
        acc_ref[...] = r_ref[...] + part

    @pl.when(f > 0)
    def _():
        acc_ref[...] = acc_ref[...] + part

    @pl.when(f == nf - 1)
    def _():
        o_ref[...] = acc_ref[...]


def ffn_down(u, state, cw, cb, wd, res, B, L, tm=512, tf=512):
    T, F2 = u.shape
    F = F2 // 2
    D = wd.shape[1]
    if L >= tm:
        nseg, sl = 1, tm
        halo = conv_halos(u, state, B, L, tm).reshape(T // tm, 1, 2, F2)
    else:
        nseg, sl = min(tm // L, B), L
        tm = nseg * sl
        halo = state.reshape(T // tm, nseg, 2, F2)
    nf = F // tf
    kern = functools.partial(_ffn_down_kernel, nseg=nseg, sl=sl, nf=nf)
    cb2 = cb.reshape(1, F2)
    return pl.pallas_call(
        kern,
        grid=(T // tm, nf),
        in_specs=[
            pl.BlockSpec((tm, tf), lambda i, f: (i, f)),
            pl.BlockSpec((tm, tf), lambda i, f: (i, nf + f)),
            pl.BlockSpec((None, nseg, 2, tf), lambda i, f: (i, 0, 0, f)),
            pl.BlockSpec((None, nseg, 2, tf), lambda i, f: (i, 0, 0, nf + f)),
            pl.BlockSpec((3, tf), lambda i, f: (0, f)),
            pl.BlockSpec((3, tf), lambda i, f: (0, nf + f)),
            pl.BlockSpec((1, tf), lambda i, f: (0, f)),
            pl.BlockSpec((1, tf), lambda i, f: (0, nf + f)),
            pl.BlockSpec((tf, D), lambda i, f: (f, 0)),
            pl.BlockSpec((tm, D), lambda i, f: (i, 0)),
        ],
        out_specs=pl.BlockSpec((tm, D), lambda i, f: (i, 0)),
        out_shape=jax.ShapeDtypeStruct((T, D), F32),
        scratch_shapes=[
            pltpu.VMEM((nseg, sl + SUBLANES, tf), F32),
            pltpu.VMEM((tm, D), F32),
        ],
        compiler_params=_params("parallel", "arbitrary"),
        name="ffn_down",
    )(u, u, halo, halo, cw, cw, cb2, cb2, wd, res)


def _last_rows(x, state, B, L):
    km1 = state.shape[1]
    x3 = x.reshape(B, L, x.shape[1])
    if L >= km1:
        return x3[:, L - km1:]
    return jnp.concatenate([state, x3], axis=1)[:, -km1:]


def trunk_layer(x, B, L, mk, mv, past_k, past_v, lru_buf, lru_h, ssd_buf, ssd_h, ffn_buf, w):
    mix = SB_HEADS * SB_HEAD_DIM
    nm = functools.partial(norm_matmul, x, w["norm_mix"])
    q = nm(w["w_q"], out_dtype=BF16)
    k = nm(w["w_k"])
    v = nm(w["w_v"])
    lx = nm(w["w_lx"])
    lg = nm(w["w_lg"])
    z = nm(w["w_z"])
    xbc = nm(w["w_xbc"])
    dt = nm(w["w_dt"])
    gates = nm(w["w_gate"])

    if past_k is None:
        k_all, v_all, P = k.reshape(B, L, mix), v.reshape(B, L, mix), 0
    else:
        P = past_k.shape[1]
        S = P + L
        pad = (-S) % LANES
        cat = lambda past, new: jnp.pad(
            jnp.concatenate([past.reshape(B, P, mix).astype(BF16), new.reshape(B, L, mix).astype(BF16)], axis=1),
            ((0, 0), (0, pad), (0, 0)))
        k_all, v_all = cat(past_k, k), cat(past_v, v)
    o_a = sb_attention(q, k_all, v_all, B, L, P)

    o_b, lru_h_new = lru_branch(lx, lg, lru_buf, lru_h, w["lru_conv_w"], w["lru_conv_b"], w["lru_wa"],
                                w["lru_ba"], w["lru_wx"], w["lru_bx"], w["lru_lam"], B, L)
    o_c, ssd_h_new = ssd_branch(z, xbc, dt, ssd_buf, ssd_h, w["ssd_conv_w"], w["ssd_conv_b"],
                                w["ssd_dt_bias"], w["ssd_a_log"], w["ssd_d"], w["ssd_norm"], B, L)
    merged = gated_merge(o_a, o_b, o_c, gates, w["w_branch"])
    h = matmul_res(merged, w["w_out"], x)
    qx = norm_matmul(h, w["norm_xattn"], w["w_xq"], out_dtype=BF16)
    o_x = cross_attention(qx, mk, mv, B, L)
    h = matmul_res(o_x, w["w_xo"], h)
    u = norm_matmul(h, w["norm_ffn"], w["w_up"])
    out = ffn_down(u, ffn_buf, w["ffn_conv_w"], w["ffn_conv_b"], w["w_down"], h, B, L)
    states = (k.reshape(B, L, SB_HEADS, SB_HEAD_DIM), v.reshape(B, L, SB_HEADS, SB_HEAD_DIM),
              _last_rows(lx, lru_buf, B, L), lru_h_new, _last_rows(xbc, ssd_buf, B, L), ssd_h_new,
              _last_rows(u, ffn_buf, B, L))
    return out, states


def kernel(x_prompt, x_sample, mem_prompt, cache_sb_k, cache_sb_v, cache_mem_k, cache_mem_v, state_lru_conv, state_lru_h, state_ssd_conv, state_ssd, state_ffn_conv, norm_mix, w_in, lru_conv_w, lru_conv_b, lru_wa, lru_ba, lru_wx, lru_bx, lru_lam, ssd_conv_w, ssd_conv_b, ssd_dt_bias, ssd_a_log, ssd_d, ssd_norm, w_branch, w_out, norm_xattn, norm_mem, w_xq, w_xk, w_xv, w_xo, norm_ffn, w_up, ffn_conv_w, ffn_conv_b, w_down, norm_final):
    depth = w_in.shape[0]
    Bp, Lp, D = x_prompt.shape
    Bs, Ls, _ = x_sample.shape
    M = mem_prompt.shape[1]
    mix = SB_HEADS * SB_HEAD_DIM
    inner = SSD_HEADS * SSD_HEAD_DIM
    cd = inner + 2 * SSD_GROUPS * SSD_STATE
    f2 = w_up.shape[2]
    xa_hd = D // XA_HEADS

    yp = x_prompt.reshape(Bp * Lp, D)
    ys = x_sample.reshape(Bs * Ls, D)
    mem = mem_prompt.reshape(Bp * M, D)
    st_p, st_s, mk_list, mv_list = [], [], [], []
    for l in range(depth):
        wi = w_in[l]
        o = 0
        cols = {}
        for name, width in (("w_q", mix), ("w_k", mix), ("w_v", mix), ("w_lx", mix), ("w_lg", mix),
                            ("w_z", inner), ("w_xbc", cd), ("w_dt", SSD_HEADS), ("w_gate", 3 * D)):
            cols[name] = wi[:, o:o + width].astype(BF16)
            o += width
        cols["w_dt"] = jnp.pad(cols["w_dt"], ((0, 0), (0, LANES - SSD_HEADS)))
        w = dict(cols, norm_mix=norm_mix[l], lru_conv_w=lru_conv_w[l], lru_conv_b=lru_conv_b[l],
                 lru_wa=lru_wa[l], lru_ba=lru_ba[l], lru_wx=lru_wx[l], lru_bx=lru_bx[l], lru_lam=lru_lam[l],
                 ssd_conv_w=ssd_conv_w[l], ssd_conv_b=ssd_conv_b[l], ssd_dt_bias=ssd_dt_bias[l],
                 ssd_a_log=ssd_a_log[l], ssd_d=ssd_d[l], ssd_norm=ssd_norm[l],
                 w_branch=w_branch[l].astype(BF16), w_out=w_out[l].astype(BF16), norm_xattn=norm_xattn[l],
                 w_xq=w_xq[l].astype(BF16), w_xo=w_xo[l].astype(BF16), norm_ffn=norm_ffn[l],
                 w_up=w_up[l].astype(BF16), ffn_conv_w=ffn_conv_w[l], ffn_conv_b=ffn_conv_b[l],
                 w_down=w_down[l].astype(BF16))
        mk_p = norm_matmul(mem, norm_mem[l], w_xk[l].astype(BF16)).reshape(Bp, M, D)
        mv_p = norm_matmul(mem, norm_mem[l], w_xv[l].astype(BF16)).reshape(Bp, M, D)
        zeros = lambda *s: jnp.zeros(s, F32)
        yp, sp = trunk_layer(yp, Bp, Lp, mk_p, mv_p, None, None, zeros(Bp, 3, mix), zeros(Bp, mix),
                             zeros(Bp, 3, cd), zeros(Bp, SSD_HEADS, SSD_HEAD_DIM, SSD_STATE),
                             zeros(Bp, 2, f2), w)
        ys, ss = trunk_layer(ys, Bs, Ls, cache_mem_k[l].reshape(Bs, M, D), cache_mem_v[l].reshape(Bs, M, D),
                             cache_sb_k[l], cache_sb_v[l], state_lru_conv[l], state_lru_h[l],
                             state_ssd_conv[l], state_ssd[l], state_ffn_conv[l], w)
        st_p.append(sp)
        st_s.append(ss)
        mk_list.append(mk_p.reshape(Bp, M, XA_HEADS, xa_hd))
        mv_list.append(mv_p.reshape(Bp, M, XA_HEADS, xa_hd))
    y_prompt = rmsnorm(yp, norm_final).reshape(Bp, Lp, D)
    y_sample = rmsnorm(ys, norm_final).reshape(Bs, Ls, D)

    def stk(lst, i):
        return jnp.stack([s[i] for s in lst], axis=0)

    return (y_prompt, y_sample,
            stk(st_p, 0), stk(st_p, 1), stk(st_p, 2), stk(st_p, 3), stk(st_p, 4), stk(st_p, 5), stk(st_p, 6),
            jnp.stack(mk_list, axis=0), jnp.stack(mv_list, axis=0),
            stk(st_s, 0), stk(st_s, 1), stk(st_s, 2), stk(st_s, 3), stk(st_s, 4), stk(st_s, 5), stk(st_s, 6))
```

```python
import functools
import math

import jax
import jax.numpy as jnp
from jax import lax
from jax.experimental import pallas as pl
from jax.experimental.pallas import tpu as pltpu

F32 = jnp.float32
BF16 = jnp.bfloat16

NORM_EPS = 1e-6
LRU_C = 8.0
LANES = 128
SUBLANES = 8
VMEM_LIMIT_BYTES = 56 * 1024 * 1024

SB_HEADS = 8
SB_HEAD_DIM = 128
LRU_BLOCKS = 8
SSD_HEADS = 16
SSD_HEAD_DIM = 64
SSD_GROUPS = 2
SSD_STATE = 128
XA_HEADS = 4


def _params(*sem):
    return pltpu.CompilerParams(dimension_semantics=sem, vmem_limit_bytes=VMEM_LIMIT_BYTES)


def _dot(a, b):
    return jnp.dot(a, b, preferred_element_type=F32)


def _dot_nt(a, b):
    return lax.dot_general(a, b, (((1,), (1,)), ((), ())), preferred_element_type=F32)


def _split3(x):
    hi = x.astype(BF16)
    r = x - hi.astype(F32)
    mid = r.astype(BF16)
    lo = (r - mid.astype(F32)).astype(BF16)
    return hi, mid, lo


def _softplus(x):
    return jnp.maximum(x, 0.0) + jnp.log1p(jnp.exp(-jnp.abs(x)))


def _silu(x):
    return x * jax.nn.sigmoid(x)


def _gelu_tanh(x):
    c = math.sqrt(2.0 / math.pi)
    return 0.5 * x * (1.0 + jnp.tanh(c * (x + 0.044715 * (x * x * x))))


def _norm_matmul_kernel(x_ref, g_ref, w_ref, o_ref, xn_ref):
    @pl.when(pl.program_id(1) == 0)
    def _():
        x = x_ref[...]
        ms = jnp.mean(x * x, axis=-1, keepdims=True)
        xn_ref[...] = (x * lax.rsqrt(ms + NORM_EPS) * g_ref[...]).astype(BF16)

    o_ref[...] = _dot(xn_ref[...], w_ref[...].astype(BF16)).astype(o_ref.dtype)


def _layer_spec(w, layer, block, index):
    if layer is None:
        return pl.BlockSpec(block, index)
    return pl.BlockSpec((None,) + block, lambda *ids: (layer,) + index(*ids))


def norm_matmul(x, g, w, layer=None, out_dtype=F32, tm=1024, tn=512):
    T, D = x.shape
    N = w.shape[-1]
    tm = min(tm, T)
    tn = min(tn, N)
    return pl.pallas_call(
        _norm_matmul_kernel,
        grid=(T // tm, N // tn),
        in_specs=[
            pl.BlockSpec((tm, D), lambda i, j: (i, 0)),
            pl.BlockSpec((1, D), lambda i, j: (0, 0)),
            _layer_spec(w, layer, (D, tn), lambda i, j: (0, j)),
        ],
        out_specs=pl.BlockSpec((tm, tn), lambda i, j: (i, j)),
        out_shape=jax.ShapeDtypeStruct((T, N), out_dtype),
        scratch_shapes=[pltpu.VMEM((tm, D), BF16)],
        compiler_params=_params("parallel", "arbitrary"),
        name="norm_matmul",
    )(x, g.reshape(1, D), w)


def _in_proj_kernel(x_ref, g_ref, w_ref, *refs, ranges):
    out_refs, xn_ref = refs[:-1], refs[-1]
    j = pl.program_id(1)

    @pl.when(j == 0)
    def _():
        x = x_ref[...]
        ms = jnp.mean(x * x, axis=-1, keepdims=True)
        xn_ref[...] = (x * lax.rsqrt(ms + NORM_EPS) * g_ref[...]).astype(BF16)

    acc = _dot(xn_ref[...], w_ref[...].astype(BF16))
    for o_ref, (lo, hi) in zip(out_refs, ranges):
        @pl.when((j >= lo) & (j < hi))
        def _(o_ref=o_ref):
            o_ref[...] = acc.astype(o_ref.dtype)


def in_proj(x, g, w_in, layer, widths, dtypes, tm=1024, tn=512):
    T, D = x.shape
    tm = min(tm, T)
    ranges, nblocks = [], 0
    for wd in widths:
        assert wd % tn == 0
        ranges.append((nblocks, nblocks + wd // tn))
        nblocks += wd // tn
    out_specs = [pl.BlockSpec((tm, tn), lambda i, j, lo=lo, n=hi - lo: (i, jnp.clip(j - lo, 0, n - 1)))
                 for lo, hi in ranges]
    return pl.pallas_call(
        functools.partial(_in_proj_kernel, ranges=tuple(ranges)),
        grid=(T // tm, nblocks),
        in_specs=[
            pl.BlockSpec((tm, D), lambda i, j: (i, 0), pipeline_mode=pl.Buffered(1)),
            pl.BlockSpec((1, D), lambda i, j: (0, 0)),
            pl.BlockSpec((None, D, tn), lambda i, j: (layer, 0, j)),
        ],
        out_specs=out_specs,
        out_shape=[jax.ShapeDtypeStruct((T, wd), dt) for wd, dt in zip(widths, dtypes)],
        scratch_shapes=[pltpu.VMEM((tm, D), BF16)],
        compiler_params=_params("parallel", "arbitrary"),
        name="in_proj",
    )(x, g.reshape(1, D), w_in)


def _matmul_res_kernel(x_ref, w_ref, r_ref, o_ref):
    o_ref[...] = r_ref[...] + _dot(x_ref[...], w_ref[...].astype(BF16))


def matmul_res(x, w, layer, res, tm=1024, tn=512):
    T, K = x.shape
    N = w.shape[-1]
    tm = min(tm, T)
    tn = min(tn, N)
    return pl.pallas_call(
        _matmul_res_kernel,
        grid=(T // tm, N // tn),
        in_specs=[
            pl.BlockSpec((tm, K), lambda i, j: (i, 0)),
            _layer_spec(w, layer, (K, tn), lambda i, j: (0, j)),
            pl.BlockSpec((tm, tn), lambda i, j: (i, j)),
        ],
        out_specs=pl.BlockSpec((tm, tn), lambda i, j: (i, j)),
        out_shape=jax.ShapeDtypeStruct((T, N), F32),
        compiler_params=_params("parallel", "arbitrary"),
        name="matmul_res",
    )(x, w, res)


def _rmsnorm_kernel(x_ref, g_ref, o_ref):
    x = x_ref[...]
    ms = jnp.mean(x * x, axis=-1, keepdims=True)
    o_ref[...] = x * lax.rsqrt(ms + NORM_EPS) * g_ref[...]


def rmsnorm(x, g, tm=512):
    T, D = x.shape
    tm = min(tm, T)
    return pl.pallas_call(
        _rmsnorm_kernel,
        grid=(T // tm,),
        in_specs=[pl.BlockSpec((tm, D), lambda i: (i, 0)), pl.BlockSpec((1, D), lambda i: (0, 0))],
        out_specs=pl.BlockSpec((tm, D), lambda i: (i, 0)),
        out_shape=jax.ShapeDtypeStruct((T, D), F32),
        compiler_params=_params("parallel"),
        name="rmsnorm",
    )(x, g.reshape(1, D))


SB_SUB = 256


def _softplus_fast(z):
    return jnp.maximum(z, 0.0) + jnp.log(1.0 + jnp.exp(-jnp.abs(z)))


def _split2(x):
    hi = x.astype(BF16)
    return hi, (x - hi.astype(F32)).astype(BF16)


def _sb_prompt_kernel(q_ref, k_ref, v_ref, o_ref, acc_ref, carry_ref, *, tq):
    qi = pl.program_id(2)
    q = q_ref[...]
    rj = lax.broadcasted_iota(jnp.int32, (SB_SUB, SB_SUB), 0)
    rs = lax.broadcasted_iota(jnp.int32, (SB_SUB, SB_SUB), 1)
    tri = jnp.where(rj >= rs, 1.0, 0.0).astype(BF16)
    scale = SB_HEAD_DIM ** -0.5
    acc_ref[...] = jnp.zeros_like(acc_ref)
    carry_ref[...] = jnp.zeros_like(carry_ref)

    def sub_block(k0, masked):
        k = k_ref[pl.ds(k0, SB_SUB), :].astype(BF16)
        v = v_ref[pl.ds(k0, SB_SUB), :].astype(BF16)
        z = _dot_nt(q, k) * scale
        sp = _softplus_fast(z)
        if masked:
            row = qi * tq + lax.broadcasted_iota(jnp.int32, (tq, SB_SUB), 0)
            col = k0 + lax.broadcasted_iota(jnp.int32, (tq, SB_SUB), 1)
            valid = col < row
            sp = jnp.where(valid, sp, 0.0)
        hi, lo = _split2(sp)
        cs = _dot(hi, tri) + _dot(lo, tri)
        w = jnp.exp(z - cs - carry_ref[...])
        if masked:
            w = jnp.where(valid, w, 0.0)
        acc_ref[...] += _dot(w.astype(BF16), v)
        carry_ref[...] += cs[:, 0:1]

    def chunk(c0, masked):
        for s in reversed(range(tq // SB_SUB)):
            sub_block(pl.multiple_of(c0 + s * SB_SUB, SB_SUB), masked)

    chunk(qi * tq, True)

    def body(it, _):
        chunk((qi - 1 - it) * tq, False)
        return 0

    lax.fori_loop(0, qi, body, 0)
    o_ref[...] = acc_ref[...].astype(o_ref.dtype)


def sb_attention_prompt(q, k, v, B, L, tq=512):
    assert L % tq == 0 and tq % SB_SUB == 0
    nq = L // tq
    return pl.pallas_call(
        functools.partial(_sb_prompt_kernel, tq=tq),
        grid=(B, SB_HEADS, nq),
        in_specs=[
            pl.BlockSpec((tq, SB_HEAD_DIM), lambda b, h, i: (b * nq + i, h)),
            pl.BlockSpec((L, SB_HEAD_DIM), lambda b, h, i: (b, h)),
            pl.BlockSpec((L, SB_HEAD_DIM), lambda b, h, i: (b, h)),
        ],
        out_specs=pl.BlockSpec((tq, SB_HEAD_DIM), lambda b, h, i: (b * nq + i, h)),
        out_shape=jax.ShapeDtypeStruct((B * L, SB_HEADS * SB_HEAD_DIM), BF16),
        scratch_shapes=[pltpu.VMEM((tq, SB_HEAD_DIM), F32), pltpu.VMEM((tq, 1), F32)],
        compiler_params=_params("parallel", "parallel", "arbitrary"),
        name="sb_attention_prompt",
    )(q, k, v)


def _sb_decode_kernel(qbd_ref, kn_ref, vn_ref, pk_ref, pv_ref, o_ref, knew_ref, vnew_ref, acc_ref,
                      *, ls, npast):
    H, D = SB_HEADS, SB_HEAD_DIM
    W = H * ls
    qbd = qbd_ref[...]
    scale = D ** -0.5
    acc_ref[...] = jnp.zeros_like(acc_ref)

    def block(k_row, v_row, nb, masked, carry):
        zt = _dot(k_row, qbd) * scale
        sp = _softplus_fast(zt)
        if masked:
            j = lax.broadcasted_iota(jnp.int32, (nb, W), 0)
            t = jnp.bitwise_and(lax.broadcasted_iota(jnp.int32, (nb, W), 1), ls - 1)
            valid = j < t
            sp = jnp.where(valid, sp, 0.0)
        rs = lax.broadcasted_iota(jnp.int32, (nb, nb), 0)
        rj = lax.broadcasted_iota(jnp.int32, (nb, nb), 1)
        tri_t = jnp.where(rj >= rs, 1.0, 0.0).astype(BF16)
        hi, lo = _split2(sp)
        cs = _dot(tri_t, hi) + _dot(tri_t, lo)
        w = jnp.exp(zt - cs - carry)
        if masked:
            w = jnp.where(valid, w, 0.0)
        acc_ref[...] += _dot(w.T.astype(BF16), v_row)
        return carry + cs[0:1, :]

    knew_ref[...] = jnp.zeros_like(knew_ref)
    vnew_ref[...] = jnp.zeros_like(vnew_ref)
    knew_ref[0:ls, :] = kn_ref[...].astype(BF16)
    vnew_ref[0:ls, :] = vn_ref[...].astype(BF16)
    carry = block(knew_ref[...], vnew_ref[...], LANES, True, jnp.zeros((1, W), F32))

    def past_rows(ref, k0):
        return jnp.concatenate(
            [ref[pl.ds(k0 * H + h, SB_SUB, stride=H), :] for h in range(H)], axis=1).astype(BF16)

    def body(it, carry):
        k0 = pl.multiple_of((npast - 1 - it) * SB_SUB, SB_SUB)
        return block(past_rows(pk_ref, k0), past_rows(pv_ref, k0), SB_SUB, False, carry)

    lax.fori_loop(0, npast, body, carry)
    for h in range(H):
        o_ref[:, h * D:(h + 1) * D] = acc_ref[h * ls:(h + 1) * ls, h * D:(h + 1) * D].astype(o_ref.dtype)


def sb_attention_decode(q, kn, vn, cache_k, cache_v, layer, B, ls):
    H, D = SB_HEADS, SB_HEAD_DIM
    depth, _, P = cache_k.shape[:3]
    W = H * ls
    assert P % SB_SUB == 0 and ls <= LANES and ls & (ls - 1) == 0 and ls % SUBLANES == 0
    qbd = jnp.einsum("bthd,hg->bhdgt", q.reshape(B, ls, H, D), jnp.eye(H, dtype=q.dtype))
    qbd = qbd.reshape(B, H * D, W)
    row = pl.BlockSpec((ls, H * D), lambda b: (b, 0))
    past = pl.BlockSpec((None, None, P * H, D), lambda b: (layer, b, 0, 0))
    return pl.pallas_call(
        functools.partial(_sb_decode_kernel, ls=ls, npast=P // SB_SUB),
        grid=(B,),
        in_specs=[pl.BlockSpec((None, H * D, W), lambda b: (b, 0, 0)), row, row, past, past],
        out_specs=row,
        out_shape=jax.ShapeDtypeStruct((B * ls, H * D), BF16),
        scratch_shapes=[pltpu.VMEM((LANES, H * D), BF16), pltpu.VMEM((LANES, H * D), BF16),
                        pltpu.VMEM((W, H * D), F32)],
        compiler_params=_params("parallel"),
        name="sb_attention_decode",
    )(qbd, kn, vn, cache_k.reshape(depth, B, P * H, D), cache_v.reshape(depth, B, P * H, D))


def _conv_taps(xp_ref, x3, halo, K, sl):
    xp_ref[:, SUBLANES - (K - 1):SUBLANES, :] = halo
    xp_ref[:, SUBLANES:SUBLANES + sl, :] = x3
    base = SUBLANES - (K - 1)
    return [xp_ref[:, base + k:base + k + sl, :] for k in range(K)]


def conv_halos(x, state, B, L, tile):
    C = x.shape[1]
    km1 = state.shape[1]
    nt = L // tile
    if nt == 1:
        return state[:, None]
    tails = x.reshape(B, nt, tile, C)[:, :-1, tile - km1:, :]
    return jnp.concatenate([state[:, None], tails], axis=1)


def _lru_kernel(lx_ref, lg_ref, halo_ref, h0_ref, cw_ref, cb_ref, wa_ref, ba_ref, wx_ref, bx_ref,
                lam_ref, o_ref, hl_ref, xp_ref, a_ref, u_ref, h_ref, hc_ref, *, tl, nl):
    i = pl.program_id(1)
    W = lx_ref.shape[1]

    @pl.when(i == 0)
    def _():
        hc_ref[...] = h0_ref[...]

    taps = _conv_taps(xp_ref, lx_ref[...][None], halo_ref[...][None], 4, tl)
    xc = cb_ref[...]
    for k in range(4):
        xc = xc + cw_ref[k:k + 1, :] * taps[k][0]

    bd = W // LRU_BLOCKS
    for n in range(LRU_BLOCKS):
        sl = slice(n * bd, (n + 1) * bd)
        xn = xc[:, sl]
        xb = xn.astype(BF16)
        r = jax.nn.sigmoid(_dot(xb, wa_ref[n].astype(BF16)) + ba_ref[:, sl])
        ig = jax.nn.sigmoid(_dot(xb, wx_ref[n].astype(BF16)) + bx_ref[:, sl])
        log_a = LRU_C * r * (-_softplus(-lam_ref[:, sl]))
        a = jnp.exp(log_a)
        u = jnp.sqrt(-jnp.tanh(log_a) * (a * a + 1.0)) * (ig * xn)
        a_ref[:, sl] = a
        u_ref[:, sl] = u

    def step(t, h):
        h = a_ref[pl.ds(t, 1), :] * h + u_ref[pl.ds(t, 1), :]
        h_ref[pl.ds(t, 1), :] = h
        return h

    h = lax.fori_loop(0, tl, step, hc_ref[...], unroll=8)
    hc_ref[...] = h
    o_ref[...] = (h_ref[...] * _gelu_tanh(lg_ref[...])).astype(o_ref.dtype)

    @pl.when(i == nl - 1)
    def _():
        hl_ref[...] = h


def lru_branch(lx, lg, state, h0, cw, cb, wa, ba, wx, bx, lam, B, L, tl=256):
    W = lx.shape[1]
    tl = min(tl, L)
    nl = L // tl
    halo = conv_halos(lx, state, B, L, tl)
    kern = functools.partial(_lru_kernel, tl=tl, nl=nl)
    vec = lambda: pl.BlockSpec((1, W), lambda b, i: (0, 0))
    blk = lambda: pl.BlockSpec(wa.shape, lambda b, i: (0, 0, 0))
    o, hl = pl.pallas_call(
        kern,
        grid=(B, nl),
        in_specs=[
            pl.BlockSpec((tl, W), lambda b, i: (b * nl + i, 0)),
            pl.BlockSpec((tl, W), lambda b, i: (b * nl + i, 0)),
            pl.BlockSpec((None, None, 3, W), lambda b, i: (b, i, 0, 0)),
            pl.BlockSpec((None, 1, W), lambda b, i: (b, 0, 0)),
            pl.BlockSpec((4, W), lambda b, i: (0, 0)),
            vec(), blk(), vec(), blk(), vec(), vec(),
        ],
        out_specs=[
            pl.BlockSpec((tl, W), lambda b, i: (b * nl + i, 0)),
            pl.BlockSpec((None, 1, W), lambda b, i: (b, 0, 0)),
        ],
        out_shape=[
            jax.ShapeDtypeStruct((B * L, W), BF16),
            jax.ShapeDtypeStruct((B, 1, W), F32),
        ],
        scratch_shapes=[
            pltpu.VMEM((1, tl + SUBLANES, W), F32),
            pltpu.VMEM((tl, W), F32),
            pltpu.VMEM((tl, W), F32),
            pltpu.VMEM((tl, W), F32),
            pltpu.VMEM((1, W), F32),
        ],
        compiler_params=_params("parallel", "arbitrary"),
        name="lru_branch",
    )(lx, lg, halo, h0.reshape(B, 1, W), cw, cb.reshape(1, W), wa, ba.reshape(1, W),
      wx, bx.reshape(1, W), lam.reshape(1, W))
    return o, hl.reshape(B, W)


def _ssd_kernel(z_ref, xbc_ref, dt_ref, halo_ref, h0_ref, cw_ref, cb_ref, dtb_ref, alog_ref,
                dexp_ref, nw_ref, o_ref, hl_ref, xp_ref, y_ref, st_ref, *, q, nc, n_valid):
    c = pl.program_id(1)
    inner = SSD_HEADS * SSD_HEAD_DIM
    gw = inner // SSD_GROUPS
    epg = SSD_HEADS // SSD_GROUPS
    n = SSD_STATE

    @pl.when(c == 0)
    def _():
        st_ref[...] = h0_ref[...]

    taps = _conv_taps(xp_ref, xbc_ref[...][None], halo_ref[...][None], 4, q)
    xc = cb_ref[...]
    for k in range(4):
        xc = xc + cw_ref[k:k + 1, :] * taps[k][0]
    xc = _silu(xc)
    xs = xc[:, :inner]
    bm = xc[:, inner:inner + SSD_GROUPS * n]
    cm = xc[:, inner + SSD_GROUPS * n:]

    rowi = lax.broadcasted_iota(jnp.int32, (q, LANES), 0)
    dts = _softplus(dt_ref[...] + dtb_ref[...])
    if n_valid < q:
        dts = jnp.where(rowi < n_valid, dts, 0.0)
    dta = dts * (-jnp.exp(alog_ref[...]))

    ti = lax.broadcasted_iota(jnp.int32, (q, q), 0)
    si = lax.broadcasted_iota(jnp.int32, (q, q), 1)
    causal = si <= ti
    lower = jnp.where(causal, 1.0, 0.0).astype(BF16)
    cum = sum(_dot(lower, p) for p in _split3(dta))
    cum_t = cum.T
    cum_last = cum[q - 1:q, :]
    exp_cum = jnp.exp(cum)
    decay_end = jnp.exp(cum_last - cum)

    eh = lax.broadcasted_iota(jnp.int32, (LANES, inner), 0)
    ec = lax.broadcasted_iota(jnp.int32, (LANES, inner), 1)
    expand = jnp.where(ec // SSD_HEAD_DIM == eh, 1.0, 0.0).astype(BF16)

    def expand_heads(a):
        return sum(_dot(p, expand) for p in _split3(a))

    dts_e = expand_heads(dts)
    exp_cum_e = expand_heads(exp_cum)
    decay_end_e = expand_heads(decay_end)
    chunk_decay_e = exp_cum_e[q - 1:q, :]

    xdt = xs * dts_e
    xdt_b = xdt.astype(BF16)
    xend_b = (decay_end_e * xdt).astype(BF16)

    for g in range(SSD_GROUPS):
        bg = bm[:, g * n:(g + 1) * n]
        cg_b = cm[:, g * n:(g + 1) * n].astype(BF16)
        gs = slice(g * gw, (g + 1) * gw)
        cb = _dot_nt(cg_b, bg.astype(BF16))
        st = st_ref[g]
        y_inter = _dot(cg_b, st.astype(BF16)) * exp_cum_e[:, gs]
        for e in range(epg):
            h = g * epg + e
            hs = slice(h * SSD_HEAD_DIM, (h + 1) * SSD_HEAD_DIM)
            seg = cum[:, h:h + 1] - cum_t[h:h + 1, :]
            decay = jnp.where(causal, jnp.exp(seg), 0.0)
            wts = (cb * decay).astype(BF16)
            y_ref[:, hs] = _dot(wts, xdt_b[:, hs]) + y_inter[:, e * SSD_HEAD_DIM:(e + 1) * SSD_HEAD_DIM]
        bg_t = bg.T.astype(BF16)
        st_ref[g] = chunk_decay_e[:, gs] * st + _dot(bg_t, xend_b[:, gs])

    y = y_ref[...] + dexp_ref[...] * xs
    yg = y * _silu(z_ref[...])
    for g in range(SSD_GROUPS):
        gs = slice(g * gw, (g + 1) * gw)
        part = yg[:, gs]
        ms = jnp.mean(part * part, axis=-1, keepdims=True)
        o_ref[:, gs] = (part * lax.rsqrt(ms + NORM_EPS) * nw_ref[:, gs]).astype(o_ref.dtype)

    @pl.when(c == nc - 1)
    def _():
        hl_ref[...] = st_ref[...]


def _pad_rows(x, B, L, Lp):
    if Lp == L:
        return x
    C = x.shape[1]
    return jnp.pad(x.reshape(B, L, C), ((0, 0), (0, Lp - L), (0, 0))).reshape(B * Lp, C)


def ssd_branch(z, xbc, dt, dt_block, state, h0, cw, cb, dt_bias, a_log, d, norm_w, B, L, q=128):
    inner = z.shape[1]
    cd = xbc.shape[1]
    nh = SSD_HEADS
    Lp = max(L, q) if L % q else L
    n_valid = min(L, q)
    nc = Lp // q
    gw = inner // SSD_GROUPS
    epg = SSD_HEADS // SSD_GROUPS
    halo = conv_halos(xbc, state, B, L, q) if L >= q else state[:, None]
    zp = _pad_rows(z, B, L, Lp)
    xbcp = _pad_rows(xbc, B, L, Lp)
    if Lp != L:
        dt, dt_block = dt[:, dt_block * LANES:(dt_block + 1) * LANES], 0
    dtp = _pad_rows(dt, B, L, Lp)
    h0t = h0.reshape(B, SSD_GROUPS, epg, SSD_HEAD_DIM, SSD_STATE).transpose(0, 1, 4, 2, 3)
    h0t = h0t.reshape(B, SSD_GROUPS, SSD_STATE, gw)
    pad1 = lambda a: jnp.pad(a, (0, LANES - nh)).reshape(1, LANES)
    kern = functools.partial(_ssd_kernel, q=q, nc=nc, n_valid=n_valid)
    row = lambda w: pl.BlockSpec((q, w), lambda b, c: (b * nc + c, 0))
    vec = lambda w: pl.BlockSpec((1, w), lambda b, c: (0, 0))
    o, hl = pl.pallas_call(
        kern,
        grid=(B, nc),
        in_specs=[
            row(inner), row(cd), pl.BlockSpec((q, LANES), lambda b, c: (b * nc + c, dt_block)),
            pl.BlockSpec((None, None, 3, cd), lambda b, c: (b, c, 0, 0)),
            pl.BlockSpec((None, SSD_GROUPS, SSD_STATE, gw), lambda b, c: (b, 0, 0, 0)),
            pl.BlockSpec((4, cd), lambda b, c: (0, 0)),
            vec(cd), vec(LANES), vec(LANES), vec(inner), vec(inner),
        ],
        out_specs=[
            row(inner),
            pl.BlockSpec((None, SSD_GROUPS, SSD_STATE, gw), lambda b, c: (b, 0, 0, 0)),
        ],
        out_shape=[
            jax.ShapeDtypeStruct((B * Lp, inner), BF16),
            jax.ShapeDtypeStruct((B, SSD_GROUPS, SSD_STATE, gw), F32),
        ],
        scratch_shapes=[
            pltpu.VMEM((1, q + SUBLANES, cd), F32),
            pltpu.VMEM((q, inner), F32),
            pltpu.VMEM((SSD_GROUPS, SSD_STATE, gw), F32),
        ],
        compiler_params=_params("parallel", "arbitrary"),
        name="ssd_branch",
    )(zp, xbcp, dtp, halo, h0t, cw, cb.reshape(1, cd), pad1(dt_bias), pad1(a_log),
      jnp.repeat(d, SSD_HEAD_DIM).reshape(1, inner), norm_w.reshape(1, inner))
    if Lp != L:
        o = o.reshape(B, Lp, inner)[:, :L].reshape(B * L, inner)
    hl = hl.reshape(B, SSD_GROUPS, SSD_STATE, epg, SSD_HEAD_DIM).transpose(0, 1, 3, 4, 2)
    return o, hl.reshape(B, SSD_HEADS, SSD_HEAD_DIM, SSD_STATE)


def _merge_kernel(a_ref, b_ref, c_ref, g0_ref, g1_ref, g2_ref, wb_ref, o_ref):
    acc = None
    for n, (br, gr) in enumerate(((a_ref, g0_ref), (b_ref, g1_ref), (c_ref, g2_ref))):
        t = _dot(br[...], wb_ref[n].astype(BF16)) * jax.nn.sigmoid(gr[...])
        acc = t if acc is None else acc + t
    o_ref[...] = acc.astype(o_ref.dtype)


def gated_merge(oa, ob, oc, gates, wb, layer, tm=1024, tn=512):
    T, K = oa.shape
    N = wb.shape[-1]
    tm = min(tm, T)
    nj = N // tn
    br = lambda: pl.BlockSpec((tm, K), lambda i, j: (i, 0))
    gate = lambda n: pl.BlockSpec((tm, tn), lambda i, j: (i, n * nj + j))
    return pl.pallas_call(
        _merge_kernel,
        grid=(T // tm, nj),
        in_specs=[br(), br(), br(), gate(0), gate(1), gate(2),
                  _layer_spec(wb, layer, (3, K, tn), lambda i, j: (0, 0, j))],
        out_specs=pl.BlockSpec((tm, tn), lambda i, j: (i, j)),
        out_shape=jax.ShapeDtypeStruct((T, N), BF16),
        compiler_params=_params("parallel", "arbitrary"),
        name="gated_merge",
    )(oa, ob, oc, gates, gates, gates, wb)


def _xattn_kernel(q_ref, mk_ref, mv_ref, o_ref):
    hd = q_ref.shape[1] // XA_HEADS
    scale = hd ** -0.5
    for h in range(XA_HEADS):
        sl = slice(h * hd, (h + 1) * hd)
        s = _dot_nt(q_ref[:, sl].astype(BF16), mk_ref[:, sl].astype(BF16)) * scale
        m = jnp.max(s, axis=-1, keepdims=True)
        p = jnp.exp(s - m)
        p = p / jnp.sum(p, axis=-1, keepdims=True)
        o_ref[:, sl] = _dot(p.astype(BF16), mv_ref[:, sl].astype(BF16)).astype(o_ref.dtype)


def cross_attention(q, mk, mv, B, L, tq=512):
    D = q.shape[1]
    M = mk.shape[1]
    tq = min(tq, L)
    nq = L // tq
    return pl.pallas_call(
        _xattn_kernel,
        grid=(B, nq),
        in_specs=[
            pl.BlockSpec((tq, D), lambda b, i: (b * nq + i, 0)),
            pl.BlockSpec((None, M, D), lambda b, i: (b, 0, 0)),
            pl.BlockSpec((None, M, D), lambda b, i: (b, 0, 0)),
        ],
        out_specs=pl.BlockSpec((tq, D), lambda b, i: (b * nq + i, 0)),
        out_shape=jax.ShapeDtypeStruct((B * L, D), BF16),
        compiler_params=_params("parallel", "arbitrary"),
        name="cross_attention",
    )(q, mk, mv)


def _ffn_down_kernel(ug_ref, uv_ref, hg_ref, hv_ref, cwg_ref, cwv_ref, cbg_ref, cbv_ref, wd_ref,
                     r_ref, o_ref, xp_ref, acc_ref, *, nseg, sl, nf):
    f = pl.program_id(1)
    tf = ug_ref.shape[1]

    def conv(u_ref, h_ref, cw_ref, cb_ref):
        taps = _conv_taps(xp_ref, u_ref[...].reshape(nseg, sl, tf), h_ref[...], 3, sl)
        y = cb_ref[...]
        for k in range(3):
            y = y + cw_ref[k:k + 1, :] * taps[k].reshape(nseg * sl, tf)
        return y

    g = conv(ug_ref, hg_ref, cwg_ref, cbg_ref)
    v = conv(uv_ref, hv_ref, cwv_ref, cbv_ref)
    part = _dot((_silu(g) * v).astype(BF16), wd_ref[...].astype(BF16))

    @pl.when(f == 0)
    def _():
        acc_ref[...] = r_ref[...] + part

    @pl.when(f > 0)
    def _():
        acc_ref[...] = acc_ref[...] + part

    @pl.when(f == nf - 1)
    def _():
        o_ref[...] = acc_ref[...]


def ffn_down(u, state, cw, cb, wd, layer, res, B, L, tm=512, tf=512):
    T, F2 = u.shape
    F = F2 // 2
    D = wd.shape[-1]
    if L >= tm:
        nseg, sl = 1, tm
        halo = conv_halos(u, state, B, L, tm).reshape(T // tm, 1, 2, F2)
    else:
        nseg, sl = min(tm // L, B), L
        tm = nseg * sl
        halo = state.reshape(T // tm, nseg, 2, F2)
    nf = F // tf
    kern = functools.partial(_ffn_down_kernel, nseg=nseg, sl=sl, nf=nf)
    cb2 = cb.reshape(1, F2)
    return pl.pallas_call(
        kern,
        grid=(T // tm, nf),
        in_specs=[
            pl.BlockSpec((tm, tf), lambda i, f: (i, f)),
            pl.BlockSpec((tm, tf), lambda i, f: (i, nf + f)),
            pl.BlockSpec((None, nseg, 2, tf), lambda i, f: (i, 0, 0, f)),
            pl.BlockSpec((None, nseg, 2, tf), lambda i, f: (i, 0, 0, nf + f)),
            pl.BlockSpec((3, tf), lambda i, f: (0, f)),
            pl.BlockSpec((3, tf), lambda i, f: (0, nf + f)),
            pl.BlockSpec((1, tf), lambda i, f: (0, f)),
            pl.BlockSpec((1, tf), lambda i, f: (0, nf + f)),
            _layer_spec(wd, layer, (tf, D), lambda i, f: (f, 0)),
            pl.BlockSpec((tm, D), lambda i, f: (i, 0)),
        ],
        out_specs=pl.BlockSpec((tm, D), lambda i, f: (i, 0)),
        out_shape=jax.ShapeDtypeStruct((T, D), F32),
        scratch_shapes=[
            pltpu.VMEM((nseg, sl + SUBLANES, tf), F32),
            pltpu.VMEM((tm, D), F32),
        ],
        compiler_params=_params("parallel", "arbitrary"),
        name="ffn_down",
    )(u, u, halo, halo, cw, cw, cb2, cb2, wd, res)


def _last_rows(x, state, B, L):
    km1 = state.shape[1]
    x3 = x.reshape(B, L, x.shape[1])
    if L >= km1:
        return x3[:, L - km1:]
    return jnp.concatenate([state, x3], axis=1)[:, -km1:]


def trunk_layer(x, B, L, mk, mv, past_k, past_v, lru_buf, lru_h, ssd_buf, ssd_h, ffn_buf, w, l):
    mix = SB_HEADS * SB_HEAD_DIM
    inner = SSD_HEADS * SSD_HEAD_DIM
    cd = inner + 2 * SSD_GROUPS * SSD_STATE
    D = x.shape[1]
    q, k, v, lx, lg, z, xbc = in_proj(x, w["norm_mix"][l], w["w_in"], l, (mix,) * 5 + (inner, cd),
                                      (BF16,) + (F32,) * 6)
    gdt = norm_matmul(x, w["norm_mix"][l], w["w_gdt"], l, tn=896)

    if past_k is None:
        o_a = sb_attention_prompt(q, k, v, B, L)
    else:
        o_a = sb_attention_decode(q, k, v, past_k, past_v, l, B, L)
    o_b, lru_h_new = lru_branch(lx, lg, lru_buf, lru_h, w["lru_conv_w"][l], w["lru_conv_b"][l], w["lru_wa"][l],
                                w["lru_ba"][l], w["lru_wx"][l], w["lru_bx"][l], w["lru_lam"][l], B, L)
    o_c, ssd_h_new = ssd_branch(z, xbc, gdt, 3 * D // LANES, ssd_buf, ssd_h, w["ssd_conv_w"][l],
                                w["ssd_conv_b"][l], w["ssd_dt_bias"][l], w["ssd_a_log"][l], w["ssd_d"][l],
                                w["ssd_norm"][l], B, L)
    merged = gated_merge(o_a, o_b, o_c, gdt, w["w_branch"], l)
    h = matmul_res(merged, w["w_out"], l, x)
    qx = norm_matmul(h, w["norm_xattn"][l], w["w_xq"], l, out_dtype=BF16)
    o_x = cross_attention(qx, mk, mv, B, L)
    h = matmul_res(o_x, w["w_xo"], l, h)
    u = norm_matmul(h, w["norm_ffn"][l], w["w_up"], l)
    out = ffn_down(u, ffn_buf, w["ffn_conv_w"][l], w["ffn_conv_b"][l], w["w_down"], l, h, B, L)
    states = (k.reshape(B, L, SB_HEADS, SB_HEAD_DIM), v.reshape(B, L, SB_HEADS, SB_HEAD_DIM),
              _last_rows(lx, lru_buf, B, L), lru_h_new, _last_rows(xbc, ssd_buf, B, L), ssd_h_new,
              _last_rows(u, ffn_buf, B, L))
    return out, states


def kernel(x_prompt, x_sample, mem_prompt, cache_sb_k, cache_sb_v, cache_mem_k, cache_mem_v, state_lru_conv, state_lru_h, state_ssd_conv, state_ssd, state_ffn_conv, norm_mix, w_in, lru_conv_w, lru_conv_b, lru_wa, lru_ba, lru_wx, lru_bx, lru_lam, ssd_conv_w, ssd_conv_b, ssd_dt_bias, ssd_a_log, ssd_d, ssd_norm, w_branch, w_out, norm_xattn, norm_mem, w_xq, w_xk, w_xv, w_xo, norm_ffn, w_up, ffn_conv_w, ffn_conv_b, w_down, norm_final):
    depth = w_in.shape[0]
    Bp, Lp, D = x_prompt.shape
    Bs, Ls, _ = x_sample.shape
    M = mem_prompt.shape[1]
    mix = SB_HEADS * SB_HEAD_DIM
    inner = SSD_HEADS * SSD_HEAD_DIM
    cd = inner + 2 * SSD_GROUPS * SSD_STATE
    f2 = w_up.shape[2]
    xa_hd = D // XA_HEADS

    yp = x_prompt.reshape(Bp * Lp, D)
    ys = x_sample.reshape(Bs * Ls, D)
    mem = mem_prompt.reshape(Bp * M, D)
    st_p, st_s, mk_list, mv_list = [], [], [], []
    main = 5 * mix + inner + cd
    w_gdt = jnp.concatenate(
        [w_in[:, :, main + SSD_HEADS:], w_in[:, :, main:main + SSD_HEADS],
         jnp.zeros((depth, D, LANES - SSD_HEADS), w_in.dtype)], axis=2)
    w = dict(w_in=w_in, w_gdt=w_gdt, norm_mix=norm_mix, lru_conv_w=lru_conv_w, lru_conv_b=lru_conv_b,
             lru_wa=lru_wa, lru_ba=lru_ba, lru_wx=lru_wx, lru_bx=lru_bx, lru_lam=lru_lam,
             ssd_conv_w=ssd_conv_w, ssd_conv_b=ssd_conv_b, ssd_dt_bias=ssd_dt_bias, ssd_a_log=ssd_a_log,
             ssd_d=ssd_d, ssd_norm=ssd_norm, w_branch=w_branch, w_out=w_out, norm_xattn=norm_xattn,
             w_xq=w_xq, w_xo=w_xo, norm_ffn=norm_ffn, w_up=w_up, ffn_conv_w=ffn_conv_w,
             ffn_conv_b=ffn_conv_b, w_down=w_down)
    for l in range(depth):
        mk_p = norm_matmul(mem, norm_mem[l], w_xk, l).reshape(Bp, M, D)
        mv_p = norm_matmul(mem, norm_mem[l], w_xv, l).reshape(Bp, M, D)
        zeros = lambda *s: jnp.zeros(s, F32)
        yp, sp = trunk_layer(yp, Bp, Lp, mk_p, mv_p, None, None, zeros(Bp, 3, mix), zeros(Bp, mix),
                             zeros(Bp, 3, cd), zeros(Bp, SSD_HEADS, SSD_HEAD_DIM, SSD_STATE),
                             zeros(Bp, 2, f2), w, l)
        ys, ss = trunk_layer(ys, Bs, Ls, cache_mem_k[l].reshape(Bs, M, D), cache_mem_v[l].reshape(Bs, M, D),
                             cache_sb_k, cache_sb_v, state_lru_conv[l], state_lru_h[l],
                             state_ssd_conv[l], state_ssd[l], state_ffn_conv[l], w, l)
        st_p.append(sp)
        st_s.append(ss)
        mk_list.append(mk_p.reshape(Bp, M, XA_HEADS, xa_hd))
        mv_list.append(mv_p.reshape(Bp, M, XA_HEADS, xa_hd))
    y_prompt = rmsnorm(yp, norm_final).reshape(Bp, Lp, D)
    y_sample = rmsnorm(ys, norm_final).reshape(Bs, Ls, D)

    def stk(lst, i):
        return jnp.stack([s[i] for s in lst], axis=0)

    return (y_prompt, y_sample,
            stk(st_p, 0), stk(st_p, 1), stk(st_p, 2), stk(st_p, 3), stk(st_p, 4), stk(st_p, 5), stk(st_p, 6),
            jnp.stack(mk_list, axis=0), jnp.stack(mv_list, axis=0),
            stk(st_s, 0), stk(st_s, 1), stk(st_s, 2), stk(st_s, 3), stk(st_s, 4), stk(st_s, 5), stk(st_s, 6))
```

```python
import functools
import math

import jax
import jax.numpy as jnp
from jax import lax
from jax.experimental import pallas as pl
from jax.experimental.pallas import tpu as pltpu

F32 = jnp.float32
BF16 = jnp.bfloat16

NORM_EPS = 1e-6
LRU_C = 8.0
LANES = 128
SUBLANES = 8
VMEM_LIMIT_BYTES = 56 * 1024 * 1024

SB_HEADS = 8
SB_HEAD_DIM = 128
LRU_BLOCKS = 8
SSD_HEADS = 16
SSD_HEAD_DIM = 64
SSD_GROUPS = 2
SSD_STATE = 128
XA_HEADS = 4


def _params(*sem):
    return pltpu.CompilerParams(dimension_semantics=sem, vmem_limit_bytes=VMEM_LIMIT_BYTES)


def _dot(a, b):
    return jnp.dot(a, b, preferred_element_type=F32)


def _dot_nt(a, b):
    return lax.dot_general(a, b, (((1,), (1,)), ((), ())), preferred_element_type=F32)


def _split3(x):
    hi = x.astype(BF16)
    r = x - hi.astype(F32)
    mid = r.astype(BF16)
    lo = (r - mid.astype(F32)).astype(BF16)
    return hi, mid, lo


def _softplus(x):
    return jnp.maximum(x, 0.0) + jnp.log1p(jnp.exp(-jnp.abs(x)))


def _silu(x):
    return x * jax.nn.sigmoid(x)


def _gelu_tanh(x):
    c = math.sqrt(2.0 / math.pi)
    return 0.5 * x * (1.0 + jnp.tanh(c * (x + 0.044715 * (x * x * x))))


def _norm_matmul_kernel(x_ref, g_ref, w_ref, o_ref, xn_ref, *, transposed):
    @pl.when(pl.program_id(1) == 0)
    def _():
        x = x_ref[...]
        ms = jnp.mean(x * x, axis=-1, keepdims=True)
        xn_ref[...] = (x * lax.rsqrt(ms + NORM_EPS) * g_ref[...]).astype(BF16)

    w = w_ref[...].astype(BF16)
    acc = _dot_nt(xn_ref[...], w) if transposed else _dot(xn_ref[...], w)
    o_ref[...] = acc.astype(o_ref.dtype)


def _layer_spec(w, layer, block, index):
    if layer is None:
        return pl.BlockSpec(block, index)
    return pl.BlockSpec((None,) + block, lambda *ids: (layer,) + index(*ids))


def norm_matmul(x, g, w, layer=None, out_dtype=F32, transposed=False, tm=1024, tn=512):
    T, D = x.shape
    N = w.shape[-2] if transposed else w.shape[-1]
    tm = min(tm, T)
    tn = min(tn, N)
    w_spec = (_layer_spec(w, layer, (tn, D), lambda i, j: (j, 0)) if transposed
              else _layer_spec(w, layer, (D, tn), lambda i, j: (0, j)))
    return pl.pallas_call(
        functools.partial(_norm_matmul_kernel, transposed=transposed),
        grid=(T // tm, N // tn),
        in_specs=[
            pl.BlockSpec((tm, D), lambda i, j: (i, 0)),
            pl.BlockSpec((1, D), lambda i, j: (0, 0)),
            w_spec,
        ],
        out_specs=pl.BlockSpec((tm, tn), lambda i, j: (i, j)),
        out_shape=jax.ShapeDtypeStruct((T, N), out_dtype),
        scratch_shapes=[pltpu.VMEM((tm, D), BF16)],
        compiler_params=_params("parallel", "arbitrary"),
        name="norm_matmul",
    )(x, g.reshape(1, D), w)


def _in_proj_kernel(x_ref, g_ref, w_ref, *refs, ranges, by_head, hd):
    out_refs, xn_ref = refs[:-1], refs[-1]
    j = pl.program_id(1)
    tm, tn = xn_ref.shape[0], w_ref.shape[0]

    @pl.when(j == 0)
    def _():
        x = x_ref[...]
        ms = jnp.mean(x * x, axis=-1, keepdims=True)
        xn_ref[...] = (x * lax.rsqrt(ms + NORM_EPS) * g_ref[...]).astype(BF16)

    acc = _dot_nt(xn_ref[...], w_ref[...].astype(BF16))
    for o_ref, (lo, hi), heads in zip(out_refs, ranges, by_head):
        @pl.when((j >= lo) & (j < hi))
        def _(o_ref=o_ref, lo=lo, heads=heads):
            if heads:
                nh = (hi - lo) * (tn // hd)
                for c in range(tn // hd):
                    o_ref[pl.ds((j - lo) * (tn // hd) + c, tm, stride=nh), :] = acc[:, c * hd:(c + 1) * hd]
            else:
                o_ref[...] = acc.astype(o_ref.dtype)


def in_proj(x, g, w_t, layer, widths, dtypes, by_head, hd, tm=1024, tn=512):
    T, D = x.shape
    tm = min(tm, T)
    ranges, nblocks = [], 0
    for wd in widths:
        assert wd % tn == 0
        ranges.append((nblocks, nblocks + wd // tn))
        nblocks += wd // tn
    out_specs, out_shape = [], []
    for (lo, hi), wd, dt, heads in zip(ranges, widths, dtypes, by_head):
        if heads:
            out_specs.append(pl.BlockSpec((tm * (wd // hd), hd), lambda i, j: (i, 0),
                                          pipeline_mode=pl.Buffered(1)))
            out_shape.append(jax.ShapeDtypeStruct((T * (wd // hd), hd), dt))
        else:
            out_specs.append(pl.BlockSpec(
                (tm, tn), lambda i, j, lo=lo, n=hi - lo: (i, jnp.clip(j - lo, 0, n - 1))))
            out_shape.append(jax.ShapeDtypeStruct((T, wd), dt))
    return pl.pallas_call(
        functools.partial(_in_proj_kernel, ranges=tuple(ranges), by_head=tuple(by_head), hd=hd),
        grid=(T // tm, nblocks),
        in_specs=[
            pl.BlockSpec((tm, D), lambda i, j: (i, 0), pipeline_mode=pl.Buffered(1)),
            pl.BlockSpec((1, D), lambda i, j: (0, 0)),
            pl.BlockSpec((None, tn, D), lambda i, j: (layer, j, 0)),
        ],
        out_specs=out_specs,
        out_shape=out_shape,
        scratch_shapes=[pltpu.VMEM((tm, D), BF16)],
        compiler_params=_params("parallel", "arbitrary"),
        name="in_proj",
    )(x, g.reshape(1, D), w_t)


def _matmul_res_kernel(x_ref, w_ref, r_ref, o_ref):
    o_ref[...] = r_ref[...] + _dot(x_ref[...], w_ref[...].astype(BF16))


def matmul_res(x, w, layer, res, tm=1024, tn=512):
    T, K = x.shape
    N = w.shape[-1]
    tm = min(tm, T)
    tn = min(tn, N)
    return pl.pallas_call(
        _matmul_res_kernel,
        grid=(T // tm, N // tn),
        in_specs=[
            pl.BlockSpec((tm, K), lambda i, j: (i, 0)),
            _layer_spec(w, layer, (K, tn), lambda i, j: (0, j)),
            pl.BlockSpec((tm, tn), lambda i, j: (i, j)),
        ],
        out_specs=pl.BlockSpec((tm, tn), lambda i, j: (i, j)),
        out_shape=jax.ShapeDtypeStruct((T, N), F32),
        compiler_params=_params("parallel", "arbitrary"),
        name="matmul_res",
    )(x, w, res)


def _rmsnorm_kernel(x_ref, g_ref, o_ref):
    x = x_ref[...]
    ms = jnp.mean(x * x, axis=-1, keepdims=True)
    o_ref[...] = x * lax.rsqrt(ms + NORM_EPS) * g_ref[...]


def rmsnorm(x, g, tm=512):
    T, D = x.shape
    tm = min(tm, T)
    return pl.pallas_call(
        _rmsnorm_kernel,
        grid=(T // tm,),
        in_specs=[pl.BlockSpec((tm, D), lambda i: (i, 0)), pl.BlockSpec((1, D), lambda i: (0, 0))],
        out_specs=pl.BlockSpec((tm, D), lambda i: (i, 0)),
        out_shape=jax.ShapeDtypeStruct((T, D), F32),
        compiler_params=_params("parallel"),
        name="rmsnorm",
    )(x, g.reshape(1, D))


SB_SUB = 256


def _softplus_fast(z):
    return jnp.maximum(z, 0.0) + jnp.log(1.0 + jnp.exp(-jnp.abs(z)))


def _split2(x):
    hi = x.astype(BF16)
    return hi, (x - hi.astype(F32)).astype(BF16)


def _sb_prompt_kernel(q_ref, k_ref, v_ref, o_ref, acc_ref, carry_ref, *, tq):
    h = pl.program_id(1)
    qi = pl.program_id(2)
    q = q_ref[...]
    rj = lax.broadcasted_iota(jnp.int32, (SB_SUB, SB_SUB), 0)
    rs = lax.broadcasted_iota(jnp.int32, (SB_SUB, SB_SUB), 1)
    tri = jnp.where(rj >= rs, 1.0, 0.0).astype(BF16)
    scale = SB_HEAD_DIM ** -0.5
    acc_ref[...] = jnp.zeros_like(acc_ref)
    carry_ref[...] = jnp.zeros_like(carry_ref)

    def chunk(c0, masked):
        c0 = pl.multiple_of(c0, tq)
        k = k_ref[pl.ds(c0 * SB_HEADS + h, tq, stride=SB_HEADS), :].astype(BF16)
        v = v_ref[pl.ds(c0 * SB_HEADS + h, tq, stride=SB_HEADS), :].astype(BF16)
        z = _dot_nt(q, k) * scale
        sp = _softplus_fast(z)
        if masked:
            row = lax.broadcasted_iota(jnp.int32, (tq, tq), 0)
            col = lax.broadcasted_iota(jnp.int32, (tq, tq), 1)
            valid = col < row
            sp = jnp.where(valid, sp, 0.0)
        hi, lo = _split2(sp)
        carry = carry_ref[...]
        args = []
        for s in reversed(range(tq // SB_SUB)):
            sl = slice(s * SB_SUB, (s + 1) * SB_SUB)
            cs = _dot(hi[:, sl], tri) + _dot(lo[:, sl], tri)
            args.append(z[:, sl] - cs - carry)
            carry = carry + cs[:, 0:1]
        carry_ref[...] = carry
        w = jnp.exp(jnp.concatenate(args[::-1], axis=1))
        if masked:
            w = jnp.where(valid, w, 0.0)
        acc_ref[...] += _dot(w.astype(BF16), v)

    chunk(qi * tq, True)

    def body(it, _):
        chunk((qi - 1 - it) * tq, False)
        return 0

    lax.fori_loop(0, qi, body, 0)
    o_ref[...] = acc_ref[...].astype(o_ref.dtype)


def sb_attention_prompt(q, k, v, B, L, tq=512):
    assert L % tq == 0 and tq % SB_SUB == 0
    nq = L // tq
    return pl.pallas_call(
        functools.partial(_sb_prompt_kernel, tq=tq),
        grid=(B, SB_HEADS, nq),
        in_specs=[
            pl.BlockSpec((tq, SB_HEAD_DIM), lambda b, h, i: (b * nq + i, h)),
            pl.BlockSpec((L * SB_HEADS, SB_HEAD_DIM), lambda b, h, i: (b, 0)),
            pl.BlockSpec((L * SB_HEADS, SB_HEAD_DIM), lambda b, h, i: (b, 0)),
        ],
        out_specs=pl.BlockSpec((tq, SB_HEAD_DIM), lambda b, h, i: (b * nq + i, h)),
        out_shape=jax.ShapeDtypeStruct((B * L, SB_HEADS * SB_HEAD_DIM), BF16),
        scratch_shapes=[pltpu.VMEM((tq, SB_HEAD_DIM), F32), pltpu.VMEM((tq, 1), F32)],
        compiler_params=_params("parallel", "parallel", "arbitrary"),
        name="sb_attention_prompt",
    )(q, k, v)


def _sb_decode_kernel(qbd_ref, kn_ref, vn_ref, pk_ref, pv_ref, o_ref, knew_ref, vnew_ref, acc_ref,
                      *, ls, npast):
    H, D = SB_HEADS, SB_HEAD_DIM
    W = H * ls
    qbd = qbd_ref[...]
    scale = D ** -0.5
    acc_ref[...] = jnp.zeros_like(acc_ref)

    def block(k_row, v_row, nb, masked, carry):
        zt = _dot(k_row, qbd) * scale
        sp = _softplus_fast(zt)
        if masked:
            j = lax.broadcasted_iota(jnp.int32, (nb, W), 0)
            t = jnp.bitwise_and(lax.broadcasted_iota(jnp.int32, (nb, W), 1), ls - 1)
            valid = j < t
            sp = jnp.where(valid, sp, 0.0)
        rs = lax.broadcasted_iota(jnp.int32, (nb, nb), 0)
        rj = lax.broadcasted_iota(jnp.int32, (nb, nb), 1)
        tri_t = jnp.where(rj >= rs, 1.0, 0.0).astype(BF16)
        hi, lo = _split2(sp)
        cs = _dot(tri_t, hi) + _dot(tri_t, lo)
        w = jnp.exp(zt - cs - carry)
        if masked:
            w = jnp.where(valid, w, 0.0)
        acc_ref[...] += _dot(w.T.astype(BF16), v_row)
        return carry + cs[0:1, :]

    def rows(ref, k0, n):
        return jnp.concatenate(
            [ref[pl.ds(k0 * H + h, n, stride=H), :] for h in range(H)], axis=1).astype(BF16)

    knew_ref[...] = jnp.zeros_like(knew_ref)
    vnew_ref[...] = jnp.zeros_like(vnew_ref)
    knew_ref[0:ls, :] = rows(kn_ref, 0, ls)
    vnew_ref[0:ls, :] = rows(vn_ref, 0, ls)
    carry = block(knew_ref[...], vnew_ref[...], LANES, True, jnp.zeros((1, W), F32))

    def body(it, carry):
        k0 = pl.multiple_of((npast - 1 - it) * SB_SUB, SB_SUB)
        return block(rows(pk_ref, k0, SB_SUB), rows(pv_ref, k0, SB_SUB), SB_SUB, False, carry)

    lax.fori_loop(0, npast, body, carry)
    for h in range(H):
        o_ref[:, h * D:(h + 1) * D] = acc_ref[h * ls:(h + 1) * ls, h * D:(h + 1) * D].astype(o_ref.dtype)


def sb_attention_decode(q, kn, vn, cache_k, cache_v, layer, B, ls):
    H, D = SB_HEADS, SB_HEAD_DIM
    depth, _, P = cache_k.shape[:3]
    W = H * ls
    assert P % SB_SUB == 0 and ls <= LANES and ls & (ls - 1) == 0 and ls % SUBLANES == 0
    qbd = jnp.einsum("bthd,hg->bhdgt", q.reshape(B, ls, H, D), jnp.eye(H, dtype=q.dtype))
    qbd = qbd.reshape(B, H * D, W)
    row = pl.BlockSpec((ls, H * D), lambda b: (b, 0))
    new = pl.BlockSpec((ls * H, D), lambda b: (b, 0))
    past = pl.BlockSpec((None, None, P * H, D), lambda b: (layer, b, 0, 0))
    return pl.pallas_call(
        functools.partial(_sb_decode_kernel, ls=ls, npast=P // SB_SUB),
        grid=(B,),
        in_specs=[pl.BlockSpec((None, H * D, W), lambda b: (b, 0, 0)), new, new, past, past],
        out_specs=row,
        out_shape=jax.ShapeDtypeStruct((B * ls, H * D), BF16),
        scratch_shapes=[pltpu.VMEM((LANES, H * D), BF16), pltpu.VMEM((LANES, H * D), BF16),
                        pltpu.VMEM((W, H * D), F32)],
        compiler_params=_params("parallel"),
        name="sb_attention_decode",
    )(qbd, kn, vn, cache_k.reshape(depth, B, P * H, D), cache_v.reshape(depth, B, P * H, D))


def _conv_taps(xp_ref, x3, halo, K, sl):
    xp_ref[:, SUBLANES - (K - 1):SUBLANES, :] = halo
    xp_ref[:, SUBLANES:SUBLANES + sl, :] = x3
    base = SUBLANES - (K - 1)
    return [xp_ref[:, base + k:base + k + sl, :] for k in range(K)]


def conv_halos(x, state, B, L, tile):
    C = x.shape[1]
    km1 = state.shape[1]
    nt = L // tile
    if nt == 1:
        return state[:, None]
    tails = x.reshape(B, nt, tile, C)[:, :-1, tile - km1:, :]
    return jnp.concatenate([state[:, None], tails], axis=1)


def _lru_kernel(lx_ref, lg_ref, halo_ref, h0_ref, cw_ref, cb_ref, wa_ref, ba_ref, wx_ref, bx_ref,
                lam_ref, o_ref, hl_ref, xp_ref, a_ref, u_ref, h_ref, hc_ref, *, tl, nl):
    i = pl.program_id(1)
    W = lx_ref.shape[1]

    @pl.when(i == 0)
    def _():
        hc_ref[...] = h0_ref[...]

    taps = _conv_taps(xp_ref, lx_ref[...][None], halo_ref[...][None], 4, tl)
    xc = cb_ref[...]
    for k in range(4):
        xc = xc + cw_ref[k:k + 1, :] * taps[k][0]

    bd = W // LRU_BLOCKS
    for n in range(LRU_BLOCKS):
        sl = slice(n * bd, (n + 1) * bd)
        xn = xc[:, sl]
        xb = xn.astype(BF16)
        r = jax.nn.sigmoid(_dot(xb, wa_ref[n].astype(BF16)) + ba_ref[:, sl])
        ig = jax.nn.sigmoid(_dot(xb, wx_ref[n].astype(BF16)) + bx_ref[:, sl])
        log_a = LRU_C * r * (-_softplus(-lam_ref[:, sl]))
        a = jnp.exp(log_a)
        u = jnp.sqrt(-jnp.tanh(log_a) * (a * a + 1.0)) * (ig * xn)
        a_ref[:, sl] = a
        u_ref[:, sl] = u

    def step(t, h):
        h = a_ref[pl.ds(t, 1), :] * h + u_ref[pl.ds(t, 1), :]
        h_ref[pl.ds(t, 1), :] = h
        return h

    h = lax.fori_loop(0, tl, step, hc_ref[...], unroll=8)
    hc_ref[...] = h
    o_ref[...] = (h_ref[...] * _gelu_tanh(lg_ref[...])).astype(o_ref.dtype)

    @pl.when(i == nl - 1)
    def _():
        hl_ref[...] = h


def lru_branch(lx, lg, state, h0, cw, cb, wa, ba, wx, bx, lam, B, L, tl=256):
    W = lx.shape[1]
    tl = min(tl, L)
    nl = L // tl
    halo = conv_halos(lx, state, B, L, tl)
    kern = functools.partial(_lru_kernel, tl=tl, nl=nl)
    vec = lambda: pl.BlockSpec((1, W), lambda b, i: (0, 0))
    blk = lambda: pl.BlockSpec(wa.shape, lambda b, i: (0, 0, 0))
    o, hl = pl.pallas_call(
        kern,
        grid=(B, nl),
        in_specs=[
            pl.BlockSpec((tl, W), lambda b, i: (b * nl + i, 0)),
            pl.BlockSpec((tl, W), lambda b, i: (b * nl + i, 0)),
            pl.BlockSpec((None, None, 3, W), lambda b, i: (b, i, 0, 0)),
            pl.BlockSpec((None, 1, W), lambda b, i: (b, 0, 0)),
            pl.BlockSpec((4, W), lambda b, i: (0, 0)),
            vec(), blk(), vec(), blk(), vec(), vec(),
        ],
        out_specs=[
            pl.BlockSpec((tl, W), lambda b, i: (b * nl + i, 0)),
            pl.BlockSpec((None, 1, W), lambda b, i: (b, 0, 0)),
        ],
        out_shape=[
            jax.ShapeDtypeStruct((B * L, W), BF16),
            jax.ShapeDtypeStruct((B, 1, W), F32),
        ],
        scratch_shapes=[
            pltpu.VMEM((1, tl + SUBLANES, W), F32),
            pltpu.VMEM((tl, W), F32),
            pltpu.VMEM((tl, W), F32),
            pltpu.VMEM((tl, W), F32),
            pltpu.VMEM((1, W), F32),
        ],
        compiler_params=_params("parallel", "arbitrary"),
        name="lru_branch",
    )(lx, lg, halo, h0.reshape(B, 1, W), cw, cb.reshape(1, W), wa, ba.reshape(1, W),
      wx, bx.reshape(1, W), lam.reshape(1, W))
    return o, hl.reshape(B, W)


def _ssd_kernel(z_ref, xbc_ref, dt_ref, halo_ref, h0_ref, cw_ref, cb_ref, dtb_ref, alog_ref,
                dexp_ref, nw_ref, o_ref, hl_ref, xp_ref, y_ref, st_ref, *, q, nc, n_valid):
    c = pl.program_id(1)
    inner = SSD_HEADS * SSD_HEAD_DIM
    gw = inner // SSD_GROUPS
    epg = SSD_HEADS // SSD_GROUPS
    n = SSD_STATE

    @pl.when(c == 0)
    def _():
        st_ref[...] = h0_ref[...]

    taps = _conv_taps(xp_ref, xbc_ref[...][None], halo_ref[...][None], 4, q)
    xc = cb_ref[...]
    for k in range(4):
        xc = xc + cw_ref[k:k + 1, :] * taps[k][0]
    xc = _silu(xc)
    xs = xc[:, :inner]
    bm = xc[:, inner:inner + SSD_GROUPS * n]
    cm = xc[:, inner + SSD_GROUPS * n:]

    rowi = lax.broadcasted_iota(jnp.int32, (q, LANES), 0)
    lanei = lax.broadcasted_iota(jnp.int32, (q, LANES), 1)
    dts = jnp.where(lanei < SSD_HEADS, _softplus(dt_ref[...] + dtb_ref[...]), 0.0)
    if n_valid < q:
        dts = jnp.where(rowi < n_valid, dts, 0.0)
    dta = dts * (-jnp.exp(alog_ref[...]))

    ti = lax.broadcasted_iota(jnp.int32, (q, q), 0)
    si = lax.broadcasted_iota(jnp.int32, (q, q), 1)
    causal = si <= ti
    lower = jnp.where(causal, 1.0, 0.0).astype(BF16)
    cum = sum(_dot(lower, p) for p in _split3(dta))
    cum_t = cum.T
    cum_last = cum[q - 1:q, :]
    exp_cum = jnp.exp(cum)
    decay_end = jnp.exp(cum_last - cum)

    eh = lax.broadcasted_iota(jnp.int32, (LANES, inner), 0)
    ec = lax.broadcasted_iota(jnp.int32, (LANES, inner), 1)
    expand = jnp.where(ec // SSD_HEAD_DIM == eh, 1.0, 0.0).astype(BF16)

    def expand_heads(a):
        return sum(_dot(p, expand) for p in _split3(a))

    dts_e = expand_heads(dts)
    exp_cum_e = expand_heads(exp_cum)
    decay_end_e = expand_heads(decay_end)
    chunk_decay_e = exp_cum_e[q - 1:q, :]

    xdt = xs * dts_e
    xdt_b = xdt.astype(BF16)
    xend_b = (decay_end_e * xdt).astype(BF16)

    for g in range(SSD_GROUPS):
        bg = bm[:, g * n:(g + 1) * n]
        cg_b = cm[:, g * n:(g + 1) * n].astype(BF16)
        gs = slice(g * gw, (g + 1) * gw)
        cb = _dot_nt(cg_b, bg.astype(BF16))
        st = st_ref[g]
        y_inter = _dot(cg_b, st.astype(BF16)) * exp_cum_e[:, gs]
        for e in range(epg):
            h = g * epg + e
            hs = slice(h * SSD_HEAD_DIM, (h + 1) * SSD_HEAD_DIM)
            seg = cum[:, h:h + 1] - cum_t[h:h + 1, :]
            decay = jnp.where(causal, jnp.exp(seg), 0.0)
            wts = (cb * decay).astype(BF16)
            y_ref[:, hs] = _dot(wts, xdt_b[:, hs]) + y_inter[:, e * SSD_HEAD_DIM:(e + 1) * SSD_HEAD_DIM]
        bg_t = bg.T.astype(BF16)
        st_ref[g] = chunk_decay_e[:, gs] * st + _dot(bg_t, xend_b[:, gs])

    y = y_ref[...] + dexp_ref[...] * xs
    yg = y * _silu(z_ref[...])
    for g in range(SSD_GROUPS):
        gs = slice(g * gw, (g + 1) * gw)
        part = yg[:, gs]
        ms = jnp.mean(part * part, axis=-1, keepdims=True)
        o_ref[:, gs] = (part * lax.rsqrt(ms + NORM_EPS) * nw_ref[:, gs]).astype(o_ref.dtype)

    @pl.when(c == nc - 1)
    def _():
        hl_ref[...] = st_ref[...]


def _pad_rows(x, B, L, Lp):
    if Lp == L:
        return x
    C = x.shape[1]
    return jnp.pad(x.reshape(B, L, C), ((0, 0), (0, Lp - L), (0, 0))).reshape(B * Lp, C)


def ssd_branch(z, xbc, dt, dt_block, state, h0, cw, cb, dt_bias, a_log, d, norm_w, B, L, q=128):
    inner = z.shape[1]
    cd = xbc.shape[1]
    nh = SSD_HEADS
    Lp = max(L, q) if L % q else L
    n_valid = min(L, q)
    nc = Lp // q
    gw = inner // SSD_GROUPS
    epg = SSD_HEADS // SSD_GROUPS
    halo = conv_halos(xbc, state, B, L, q) if L >= q else state[:, None]
    zp = _pad_rows(z, B, L, Lp)
    xbcp = _pad_rows(xbc, B, L, Lp)
    if Lp != L:
        dt, dt_block = dt[:, dt_block * LANES:(dt_block + 1) * LANES], 0
    dtp = _pad_rows(dt, B, L, Lp)
    h0t = h0.reshape(B, SSD_GROUPS, epg, SSD_HEAD_DIM, SSD_STATE).transpose(0, 1, 4, 2, 3)
    h0t = h0t.reshape(B, SSD_GROUPS, SSD_STATE, gw)
    pad1 = lambda a: jnp.pad(a, (0, LANES - nh)).reshape(1, LANES)
    kern = functools.partial(_ssd_kernel, q=q, nc=nc, n_valid=n_valid)
    row = lambda w: pl.BlockSpec((q, w), lambda b, c: (b * nc + c, 0))
    vec = lambda w: pl.BlockSpec((1, w), lambda b, c: (0, 0))
    o, hl = pl.pallas_call(
        kern,
        grid=(B, nc),
        in_specs=[
            row(inner), row(cd), pl.BlockSpec((q, LANES), lambda b, c: (b * nc + c, dt_block)),
            pl.BlockSpec((None, None, 3, cd), lambda b, c: (b, c, 0, 0)),
            pl.BlockSpec((None, SSD_GROUPS, SSD_STATE, gw), lambda b, c: (b, 0, 0, 0)),
            pl.BlockSpec((4, cd), lambda b, c: (0, 0)),
            vec(cd), vec(LANES), vec(LANES), vec(inner), vec(inner),
        ],
        out_specs=[
            row(inner),
            pl.BlockSpec((None, SSD_GROUPS, SSD_STATE, gw), lambda b, c: (b, 0, 0, 0)),
        ],
        out_shape=[
            jax.ShapeDtypeStruct((B * Lp, inner), BF16),
            jax.ShapeDtypeStruct((B, SSD_GROUPS, SSD_STATE, gw), F32),
        ],
        scratch_shapes=[
            pltpu.VMEM((1, q + SUBLANES, cd), F32),
            pltpu.VMEM((q, inner), F32),
            pltpu.VMEM((SSD_GROUPS, SSD_STATE, gw), F32),
        ],
        compiler_params=_params("parallel", "arbitrary"),
        name="ssd_branch",
    )(zp, xbcp, dtp, halo, h0t, cw, cb.reshape(1, cd), pad1(dt_bias), pad1(a_log),
      jnp.repeat(d, SSD_HEAD_DIM).reshape(1, inner), norm_w.reshape(1, inner))
    if Lp != L:
        o = o.reshape(B, Lp, inner)[:, :L].reshape(B * L, inner)
    hl = hl.reshape(B, SSD_GROUPS, SSD_STATE, epg, SSD_HEAD_DIM).transpose(0, 1, 3, 4, 2)
    return o, hl.reshape(B, SSD_HEADS, SSD_HEAD_DIM, SSD_STATE)


def _merge_kernel(a_ref, b_ref, c_ref, g0_ref, g1_ref, g2_ref, wb_ref, o_ref):
    acc = None
    for n, (br, gr) in enumerate(((a_ref, g0_ref), (b_ref, g1_ref), (c_ref, g2_ref))):
        t = _dot(br[...], wb_ref[n].astype(BF16)) * jax.nn.sigmoid(gr[...])
        acc = t if acc is None else acc + t
    o_ref[...] = acc.astype(o_ref.dtype)


def gated_merge(oa, ob, oc, gates, wb, layer, tm=1024, tn=512):
    T, K = oa.shape
    N = wb.shape[-1]
    tm = min(tm, T)
    nj = N // tn
    br = lambda: pl.BlockSpec((tm, K), lambda i, j: (i, 0))
    gate = lambda n: pl.BlockSpec((tm, tn), lambda i, j: (i, n * nj + j))
    return pl.pallas_call(
        _merge_kernel,
        grid=(T // tm, nj),
        in_specs=[br(), br(), br(), gate(0), gate(1), gate(2),
                  _layer_spec(wb, layer, (3, K, tn), lambda i, j: (0, 0, j))],
        out_specs=pl.BlockSpec((tm, tn), lambda i, j: (i, j)),
        out_shape=jax.ShapeDtypeStruct((T, N), BF16),
        compiler_params=_params("parallel", "arbitrary"),
        name="gated_merge",
    )(oa, ob, oc, gates, gates, gates, wb)


def _xattn_kernel(q_ref, mk_ref, mv_ref, o_ref):
    hd = mk_ref.shape[2]
    scale = hd ** -0.5
    for h in range(XA_HEADS):
        sl = slice(h * hd, (h + 1) * hd)
        s = _dot_nt(q_ref[:, sl].astype(BF16), mk_ref[:, h, :].astype(BF16)) * scale
        m = jnp.max(s, axis=-1, keepdims=True)
        p = jnp.exp(s - m)
        p = p / jnp.sum(p, axis=-1, keepdims=True)
        o_ref[:, sl] = _dot(p.astype(BF16), mv_ref[:, h, :].astype(BF16)).astype(o_ref.dtype)


def cross_attention(q, mk, mv, layer, B, L, tq=512):
    D = q.shape[1]
    M, H, hd = mk.shape[2:]
    tq = min(tq, L)
    nq = L // tq
    mem = lambda: pl.BlockSpec((None, None, M, H, hd), lambda b, i: (layer, b, 0, 0, 0))
    return pl.pallas_call(
        _xattn_kernel,
        grid=(B, nq),
        in_specs=[pl.BlockSpec((tq, D), lambda b, i: (b * nq + i, 0)), mem(), mem()],
        out_specs=pl.BlockSpec((tq, D), lambda b, i: (b * nq + i, 0)),
        out_shape=jax.ShapeDtypeStruct((B * L, D), BF16),
        compiler_params=_params("parallel", "arbitrary"),
        name="cross_attention",
    )(q, mk, mv)


def _mem_kv_kernel(x_ref, g_ref, wk_ref, wv_ref, k_ref, v_ref, xn_ref):
    j = pl.program_id(1)

    @pl.when(j == 0)
    def _():
        x = x_ref[...]
        ms = jnp.mean(x * x, axis=-1, keepdims=True)
        xn_ref[...] = (x * lax.rsqrt(ms + NORM_EPS) * g_ref[...]).astype(BF16)

    xn = xn_ref[...]
    k = _dot(xn, wk_ref[...].astype(BF16))
    v = _dot(xn, wv_ref[...].astype(BF16))
    for h in range(XA_HEADS):
        @pl.when(j == h)
        def _(h=h):
            k_ref[:, h, :] = k
            v_ref[:, h, :] = v


def mem_kv(mem, g, wk, wv, layer, B, M):
    D = mem.shape[1]
    hd = D // XA_HEADS
    w = lambda: pl.BlockSpec((None, D, hd), lambda b, j: (layer, 0, j))
    o = lambda: pl.BlockSpec((None, M, XA_HEADS, hd), lambda b, j: (b, 0, 0, 0))
    return pl.pallas_call(
        _mem_kv_kernel,
        grid=(B, XA_HEADS),
        in_specs=[pl.BlockSpec((M, D), lambda b, j: (b, 0)), pl.BlockSpec((1, D), lambda b, j: (0, 0)),
                  w(), w()],
        out_specs=[o(), o()],
        out_shape=[jax.ShapeDtypeStruct((B, M, XA_HEADS, hd), F32)] * 2,
        scratch_shapes=[pltpu.VMEM((M, D), BF16)],
        compiler_params=_params("parallel", "arbitrary"),
        name="mem_kv",
    )(mem, g.reshape(1, D), wk, wv)


def _ffn_up_kernel(x_ref, g_ref, wg_ref, wv_ref, sg_ref, sv_ref, cwg_ref, cwv_ref, cbg_ref, cbv_ref,
                   act_ref, tg_ref, tv_ref, xn_ref, xp_ref, carry_ref, *, nseg, sl, tpb):
    i = pl.program_id(0)
    f = pl.program_id(1)
    tm, tf = act_ref.shape

    @pl.when(f == 0)
    def _():
        x = x_ref[...]
        ms = jnp.mean(x * x, axis=-1, keepdims=True)
        xn_ref[...] = (x * lax.rsqrt(ms + NORM_EPS) * g_ref[...]).astype(BF16)

    xn = xn_ref[...]
    base = SUBLANES - 2

    def half(c, w_ref, st_ref, cw_ref, cb_ref, tail_ref):
        u3 = _dot(xn, w_ref[...].astype(BF16)).reshape(nseg, sl, tf)
        if tpb == 1:
            xp_ref[:, base:SUBLANES, :] = st_ref[...]
        else:
            @pl.when(i % tpb == 0)
            def _():
                xp_ref[:, base:SUBLANES, :] = st_ref[...]

            @pl.when(i % tpb != 0)
            def _():
                xp_ref[:, base:SUBLANES, :] = carry_ref[c, f]
        xp_ref[:, SUBLANES:SUBLANES + sl, :] = u3
        y = cb_ref[...]
        for k in range(3):
            y = y + cw_ref[k:k + 1, :] * xp_ref[:, base + k:base + k + sl, :].reshape(tm, tf)
        tail = xp_ref[:, SUBLANES + sl - 2:SUBLANES + sl, :]
        if tpb > 1:
            carry_ref[c, f] = tail
        tail_ref[...] = tail
        return y

    gate = half(0, wg_ref, sg_ref, cwg_ref, cbg_ref, tg_ref)
    val = half(1, wv_ref, sv_ref, cwv_ref, cbv_ref, tv_ref)
    act_ref[...] = (_silu(gate) * val).astype(act_ref.dtype)


def ffn_up(x, g, w_up, layer, state, cw, cb, B, L, tm=1024, tf=512):
    T, D = x.shape
    F2 = w_up.shape[-1]
    F = F2 // 2
    if L >= tm:
        nseg, sl, tpb = 1, tm, L // tm
    else:
        nseg, sl, tpb = min(tm // L, B), L, 1
        tm = nseg * sl
    nt, nf = T // tm, F // tf
    st = state.reshape(B // nseg, nseg, 2, F2)
    cb2 = cb.reshape(1, F2)
    kern = functools.partial(_ffn_up_kernel, nseg=nseg, sl=sl, tpb=tpb)
    wspec = lambda off: pl.BlockSpec((None, D, tf), lambda i, f: (layer, 0, off + f))
    sspec = lambda off: pl.BlockSpec((None, nseg, 2, tf), lambda i, f: (i // tpb, 0, 0, off + f))
    cwspec = lambda off: pl.BlockSpec((3, tf), lambda i, f: (0, off + f))
    cbspec = lambda off: pl.BlockSpec((1, tf), lambda i, f: (0, off + f))
    tspec = lambda: pl.BlockSpec((None, nseg, 2, tf), lambda i, f: (i, 0, 0, f))
    act, tg, tv = pl.pallas_call(
        kern,
        grid=(nt, nf),
        in_specs=[
            pl.BlockSpec((tm, D), lambda i, f: (i, 0), pipeline_mode=pl.Buffered(1)),
            pl.BlockSpec((1, D), lambda i, f: (0, 0)),
            wspec(0), wspec(nf), sspec(0), sspec(nf), cwspec(0), cwspec(nf), cbspec(0), cbspec(nf),
        ],
        out_specs=[pl.BlockSpec((tm, tf), lambda i, f: (i, f)), tspec(), tspec()],
        out_shape=[
            jax.ShapeDtypeStruct((T, F), BF16),
            jax.ShapeDtypeStruct((nt, nseg, 2, F), F32),
            jax.ShapeDtypeStruct((nt, nseg, 2, F), F32),
        ],
        scratch_shapes=[
            pltpu.VMEM((tm, D), BF16),
            pltpu.VMEM((nseg, sl + SUBLANES, tf), F32),
            pltpu.VMEM((2, nf, nseg, 2, tf), F32),
        ],
        compiler_params=_params("arbitrary", "arbitrary"),
        name="ffn_up",
    )(x, g.reshape(1, D), w_up, w_up, st, st, cw, cw, cb2, cb2)
    last = lambda t: t.reshape(B // nseg, tpb, nseg, 2, F)[:, -1].reshape(B, 2, F)
    return act, jnp.concatenate([last(tg), last(tv)], axis=-1)


def _last_rows(x, state, B, L):
    km1 = state.shape[1]
    x3 = x.reshape(B, L, x.shape[1])
    if L >= km1:
        return x3[:, L - km1:]
    return jnp.concatenate([state, x3], axis=1)[:, -km1:]


def trunk_layer(x, B, L, mk, mv, mem_layer, past_k, past_v, lru_buf, lru_h, ssd_buf, ssd_h, ffn_buf, w, l):
    mix = SB_HEADS * SB_HEAD_DIM
    inner = SSD_HEADS * SSD_HEAD_DIM
    cd = inner + 2 * SSD_GROUPS * SSD_STATE
    D = x.shape[1]
    tn = 512
    q, k, v, lx, lg, z, xbc, dtx = in_proj(x, w["norm_mix"][l], w["w_in_t"], l,
                                           (mix,) * 5 + (inner, cd, tn), (BF16,) + (F32,) * 7,
                                           (False, True, True) + (False,) * 5, SB_HEAD_DIM, tn=tn)
    gates = norm_matmul(x, w["norm_mix"][l], w["w_gate_t"], l, transposed=True)

    if past_k is None:
        o_a = sb_attention_prompt(q, k, v, B, L)
    else:
        o_a = sb_attention_decode(q, k, v, past_k, past_v, l, B, L)
    o_b, lru_h_new = lru_branch(lx, lg, lru_buf, lru_h, w["lru_conv_w"][l], w["lru_conv_b"][l], w["lru_wa"][l],
                                w["lru_ba"][l], w["lru_wx"][l], w["lru_bx"][l], w["lru_lam"][l], B, L)
    o_c, ssd_h_new = ssd_branch(z, xbc, dtx, 0, ssd_buf, ssd_h, w["ssd_conv_w"][l],
                                w["ssd_conv_b"][l], w["ssd_dt_bias"][l], w["ssd_a_log"][l], w["ssd_d"][l],
                                w["ssd_norm"][l], B, L)
    merged = gated_merge(o_a, o_b, o_c, gates, w["w_branch"], l)
    h = matmul_res(merged, w["w_out"], l, x)
    qx = norm_matmul(h, w["norm_xattn"][l], w["w_xq"], l, out_dtype=BF16)
    o_x = cross_attention(qx, mk, mv, mem_layer, B, L)
    h = matmul_res(o_x, w["w_xo"], l, h)
    act, ffn_new = ffn_up(h, w["norm_ffn"][l], w["w_up"], l, ffn_buf, w["ffn_conv_w"][l],
                          w["ffn_conv_b"][l], B, L)
    out = matmul_res(act, w["w_down"], l, h, tn=256)
    states = (k.reshape(B, L, SB_HEADS, SB_HEAD_DIM), v.reshape(B, L, SB_HEADS, SB_HEAD_DIM),
              _last_rows(lx, lru_buf, B, L), lru_h_new, _last_rows(xbc, ssd_buf, B, L), ssd_h_new,
              ffn_new)
    return out, states


def kernel(x_prompt, x_sample, mem_prompt, cache_sb_k, cache_sb_v, cache_mem_k, cache_mem_v, state_lru_conv, state_lru_h, state_ssd_conv, state_ssd, state_ffn_conv, norm_mix, w_in, lru_conv_w, lru_conv_b, lru_wa, lru_ba, lru_wx, lru_bx, lru_lam, ssd_conv_w, ssd_conv_b, ssd_dt_bias, ssd_a_log, ssd_d, ssd_norm, w_branch, w_out, norm_xattn, norm_mem, w_xq, w_xk, w_xv, w_xo, norm_ffn, w_up, ffn_conv_w, ffn_conv_b, w_down, norm_final):
    depth = w_in.shape[0]
    Bp, Lp, D = x_prompt.shape
    Bs, Ls, _ = x_sample.shape
    M = mem_prompt.shape[1]
    mix = SB_HEADS * SB_HEAD_DIM
    inner = SSD_HEADS * SSD_HEAD_DIM
    cd = inner + 2 * SSD_GROUPS * SSD_STATE
    f2 = w_up.shape[2]
    xa_hd = D // XA_HEADS

    yp = x_prompt.reshape(Bp * Lp, D)
    ys = x_sample.reshape(Bs * Ls, D)
    mem = mem_prompt.reshape(Bp * M, D)
    st_p, st_s, mk_list, mv_list = [], [], [], []
    w_in_t = jnp.swapaxes(w_in, 1, 2)
    w_gate_t = w_in_t[:, 5 * mix + inner + cd + SSD_HEADS:]
    w = dict(w_in_t=w_in_t, w_gate_t=w_gate_t, norm_mix=norm_mix, lru_conv_w=lru_conv_w, lru_conv_b=lru_conv_b,
             lru_wa=lru_wa, lru_ba=lru_ba, lru_wx=lru_wx, lru_bx=lru_bx, lru_lam=lru_lam,
             ssd_conv_w=ssd_conv_w, ssd_conv_b=ssd_conv_b, ssd_dt_bias=ssd_dt_bias, ssd_a_log=ssd_a_log,
             ssd_d=ssd_d, ssd_norm=ssd_norm, w_branch=w_branch, w_out=w_out, norm_xattn=norm_xattn,
             w_xq=w_xq, w_xo=w_xo, norm_ffn=norm_ffn, w_up=w_up, ffn_conv_w=ffn_conv_w,
             ffn_conv_b=ffn_conv_b, w_down=w_down)
    for l in range(depth):
        mk_p, mv_p = mem_kv(mem, norm_mem[l], w_xk, w_xv, l, Bp, M)
        zeros = lambda *s: jnp.zeros(s, F32)
        yp, sp = trunk_layer(yp, Bp, Lp, mk_p[None], mv_p[None], 0, None, None, zeros(Bp, 3, mix),
                             zeros(Bp, mix), zeros(Bp, 3, cd), zeros(Bp, SSD_HEADS, SSD_HEAD_DIM, SSD_STATE),
                             zeros(Bp, 2, f2), w, l)
        ys, ss = trunk_layer(ys, Bs, Ls, cache_mem_k, cache_mem_v, l,
                             cache_sb_k, cache_sb_v, state_lru_conv[l], state_lru_h[l],
                             state_ssd_conv[l], state_ssd[l], state_ffn_conv[l], w, l)
        st_p.append(sp)
        st_s.append(ss)
        mk_list.append(mk_p)
        mv_list.append(mv_p)
    y_prompt = rmsnorm(yp, norm_final).reshape(Bp, Lp, D)
    y_sample = rmsnorm(ys, norm_final).reshape(Bs, Ls, D)

    def stk(lst, i):
        return jnp.stack([s[i] for s in lst], axis=0)

    return (y_prompt, y_sample,
            stk(st_p, 0), stk(st_p, 1), stk(st_p, 2), stk(st_p, 3), stk(st_p, 4), stk(st_p, 5), stk(st_p, 6),
            jnp.stack(mk_list, axis=0), jnp.stack(mv_list, axis=0),
            stk(st_s, 0), stk(st_s, 1), stk(st_s, 2), stk(st_s, 3), stk(st_s, 4), stk(st_s, 5), stk(st_s, 6))
```

```python
import functools
import math

import jax
import jax.numpy as jnp
from jax import lax
from jax.experimental import pallas as pl
from jax.experimental.pallas import tpu as pltpu

F32 = jnp.float32
BF16 = jnp.bfloat16

NORM_EPS = 1e-6
LRU_C = 8.0
LANES = 128
SUBLANES = 8
MXU_WIDTH = 256
VMEM_LIMIT_BYTES = 56 * 1024 * 1024

SB_HEADS = 8
SB_HEAD_DIM = 128
LRU_BLOCKS = 8
SSD_HEADS = 16
SSD_HEAD_DIM = 64
SSD_GROUPS = 2
SSD_STATE = 128
XA_HEADS = 4


def _params(*sem):
    return pltpu.CompilerParams(dimension_semantics=sem, vmem_limit_bytes=VMEM_LIMIT_BYTES)


def _dot(a, b):
    return jnp.dot(a, b, preferred_element_type=F32)


def _dot_nt(a, b):
    return lax.dot_general(a, b, (((1,), (1,)), ((), ())), preferred_element_type=F32)


def _split3(x):
    hi = x.astype(BF16)
    r = x - hi.astype(F32)
    mid = r.astype(BF16)
    lo = (r - mid.astype(F32)).astype(BF16)
    return hi, mid, lo


def _softplus(x):
    return jnp.maximum(x, 0.0) + jnp.log1p(jnp.exp(-jnp.abs(x)))


def _silu(x):
    return x * jax.nn.sigmoid(x)


def _gelu_tanh(x):
    c = math.sqrt(2.0 / math.pi)
    return 0.5 * x * (1.0 + jnp.tanh(c * (x + 0.044715 * (x * x * x))))


def _norm_matmul_kernel(x_ref, g_ref, w_ref, o_ref, xn_ref, *, transposed):
    @pl.when(pl.program_id(1) == 0)
    def _():
        x = x_ref[...]
        ms = jnp.mean(x * x, axis=-1, keepdims=True)
        xn_ref[...] = (x * lax.rsqrt(ms + NORM_EPS) * g_ref[...]).astype(BF16)

    w = w_ref[...].astype(BF16)
    acc = _dot_nt(xn_ref[...], w) if transposed else _dot(xn_ref[...], w)
    o_ref[...] = acc.astype(o_ref.dtype)


def _layer_spec(w, layer, block, index):
    if layer is None:
        return pl.BlockSpec(block, index)
    return pl.BlockSpec((None,) + block, lambda *ids: (layer,) + index(*ids))


def norm_matmul(x, g, w, layer=None, out_dtype=F32, transposed=False, tm=1024, tn=512):
    T, D = x.shape
    N = w.shape[-2] if transposed else w.shape[-1]
    tm = min(tm, T)
    tn = min(tn, N)
    w_spec = (_layer_spec(w, layer, (tn, D), lambda i, j: (j, 0)) if transposed
              else _layer_spec(w, layer, (D, tn), lambda i, j: (0, j)))
    return pl.pallas_call(
        functools.partial(_norm_matmul_kernel, transposed=transposed),
        grid=(T // tm, N // tn),
        in_specs=[
            pl.BlockSpec((tm, D), lambda i, j: (i, 0)),
            pl.BlockSpec((1, D), lambda i, j: (0, 0)),
            w_spec,
        ],
        out_specs=pl.BlockSpec((tm, tn), lambda i, j: (i, j)),
        out_shape=jax.ShapeDtypeStruct((T, N), out_dtype),
        scratch_shapes=[pltpu.VMEM((tm, D), BF16)],
        compiler_params=_params("parallel", "arbitrary"),
        name="norm_matmul",
    )(x, g.reshape(1, D), w)


def _in_proj_kernel(x_ref, g_ref, w_ref, *refs, ranges, by_head, hd):
    out_refs, xn_ref = refs[:-1], refs[-1]
    j = pl.program_id(1)
    tm, tn = xn_ref.shape[0], w_ref.shape[0]

    @pl.when(j == 0)
    def _():
        x = x_ref[...]
        ms = jnp.mean(x * x, axis=-1, keepdims=True)
        xn_ref[...] = (x * lax.rsqrt(ms + NORM_EPS) * g_ref[...]).astype(BF16)

    acc = _dot_nt(xn_ref[...], w_ref[...].astype(BF16))
    for o_ref, (lo, hi), heads in zip(out_refs, ranges, by_head):
        @pl.when((j >= lo) & (j < hi))
        def _(o_ref=o_ref, lo=lo, heads=heads):
            if heads:
                nh = (hi - lo) * (tn // hd)
                for c in range(tn // hd):
                    o_ref[pl.ds((j - lo) * (tn // hd) + c, tm, stride=nh), :] = acc[:, c * hd:(c + 1) * hd]
            else:
                o_ref[...] = acc.astype(o_ref.dtype)


def in_proj(x, g, w_t, layer, widths, dtypes, by_head, hd, tm=1024, tn=512):
    T, D = x.shape
    tm = min(tm, T)
    ranges, nblocks = [], 0
    for wd in widths:
        assert wd % tn == 0
        ranges.append((nblocks, nblocks + wd // tn))
        nblocks += wd // tn
    out_specs, out_shape = [], []
    for (lo, hi), wd, dt, heads in zip(ranges, widths, dtypes, by_head):
        if heads:
            out_specs.append(pl.BlockSpec((tm * (wd // hd), hd), lambda i, j: (i, 0),
                                          pipeline_mode=pl.Buffered(1)))
            out_shape.append(jax.ShapeDtypeStruct((T * (wd // hd), hd), dt))
        else:
            out_specs.append(pl.BlockSpec(
                (tm, tn), lambda i, j, lo=lo, n=hi - lo: (i, jnp.clip(j - lo, 0, n - 1))))
            out_shape.append(jax.ShapeDtypeStruct((T, wd), dt))
    return pl.pallas_call(
        functools.partial(_in_proj_kernel, ranges=tuple(ranges), by_head=tuple(by_head), hd=hd),
        grid=(T // tm, nblocks),
        in_specs=[
            pl.BlockSpec((tm, D), lambda i, j: (i, 0), pipeline_mode=pl.Buffered(1)),
            pl.BlockSpec((1, D), lambda i, j: (0, 0)),
            pl.BlockSpec((None, tn, D), lambda i, j: (layer, j, 0)),
        ],
        out_specs=out_specs,
        out_shape=out_shape,
        scratch_shapes=[pltpu.VMEM((tm, D), BF16)],
        compiler_params=_params("parallel", "arbitrary"),
        name="in_proj",
    )(x, g.reshape(1, D), w_t)


def _matmul_res_kernel(x_ref, w_ref, r_ref, o_ref):
    o_ref[...] = r_ref[...] + _dot(x_ref[...], w_ref[...].astype(BF16))


def matmul_res(x, w, layer, res, tm=1024, tn=512):
    T, K = x.shape
    N = w.shape[-1]
    tm = min(tm, T)
    tn = min(tn, N)
    return pl.pallas_call(
        _matmul_res_kernel,
        grid=(T // tm, N // tn),
        in_specs=[
            pl.BlockSpec((tm, K), lambda i, j: (i, 0)),
            _layer_spec(w, layer, (K, tn), lambda i, j: (0, j)),
            pl.BlockSpec((tm, tn), lambda i, j: (i, j)),
        ],
        out_specs=pl.BlockSpec((tm, tn), lambda i, j: (i, j)),
        out_shape=jax.ShapeDtypeStruct((T, N), F32),
        compiler_params=_params("parallel", "arbitrary"),
        name="matmul_res",
    )(x, w, res)


def _rmsnorm_kernel(x_ref, g_ref, o_ref):
    x = x_ref[...]
    ms = jnp.mean(x * x, axis=-1, keepdims=True)
    o_ref[...] = x * lax.rsqrt(ms + NORM_EPS) * g_ref[...]


def rmsnorm(x, g, tm=512):
    T, D = x.shape
    tm = min(tm, T)
    return pl.pallas_call(
        _rmsnorm_kernel,
        grid=(T // tm,),
        in_specs=[pl.BlockSpec((tm, D), lambda i: (i, 0)), pl.BlockSpec((1, D), lambda i: (0, 0))],
        out_specs=pl.BlockSpec((tm, D), lambda i: (i, 0)),
        out_shape=jax.ShapeDtypeStruct((T, D), F32),
        compiler_params=_params("parallel"),
        name="rmsnorm",
    )(x, g.reshape(1, D))


SB_SUB = MXU_WIDTH


def _softplus_fast(z):
    return jnp.maximum(z, 0.0) + jnp.log(1.0 + jnp.exp(-jnp.abs(z)))


def _split2(x):
    hi = x.astype(BF16)
    return hi, (x - hi.astype(F32)).astype(BF16)


def _sb_prompt_kernel(q_ref, k_ref, v_ref, o_ref, acc_ref, carry_ref, *, tq):
    h = pl.program_id(1)
    qi = pl.program_id(2)
    q = q_ref[...]
    rj = lax.broadcasted_iota(jnp.int32, (SB_SUB, SB_SUB), 0)
    rs = lax.broadcasted_iota(jnp.int32, (SB_SUB, SB_SUB), 1)
    tri = jnp.where(rj >= rs, 1.0, 0.0).astype(BF16)
    scale = SB_HEAD_DIM ** -0.5
    acc_ref[...] = jnp.zeros_like(acc_ref)
    carry_ref[...] = jnp.zeros_like(carry_ref)

    def chunk(c0, masked):
        c0 = pl.multiple_of(c0, tq)
        k = k_ref[pl.ds(c0 * SB_HEADS + h, tq, stride=SB_HEADS), :].astype(BF16)
        v = v_ref[pl.ds(c0 * SB_HEADS + h, tq, stride=SB_HEADS), :].astype(BF16)
        z = _dot_nt(q, k) * scale
        sp = _softplus_fast(z)
        if masked:
            row = lax.broadcasted_iota(jnp.int32, (tq, tq), 0)
            col = lax.broadcasted_iota(jnp.int32, (tq, tq), 1)
            valid = col < row
            sp = jnp.where(valid, sp, 0.0)
        hi, lo = _split2(sp)
        carry = carry_ref[...]
        args = []
        for s in reversed(range(tq // SB_SUB)):
            sl = slice(s * SB_SUB, (s + 1) * SB_SUB)
            cs = _dot(hi[:, sl], tri) + _dot(lo[:, sl], tri)
            args.append(z[:, sl] - cs - carry)
            carry = carry + cs[:, 0:1]
        carry_ref[...] = carry
        w = jnp.exp(jnp.concatenate(args[::-1], axis=1))
        if masked:
            w = jnp.where(valid, w, 0.0)
        acc_ref[...] += _dot(w.astype(BF16), v)

    chunk(qi * tq, True)

    def body(it, _):
        chunk((qi - 1 - it) * tq, False)
        return 0

    lax.fori_loop(0, qi, body, 0)
    o_ref[...] = acc_ref[...].astype(o_ref.dtype)


def sb_attention_prompt(q, k, v, B, L, tq=512):
    assert L % tq == 0 and tq % SB_SUB == 0
    nq = L // tq
    return pl.pallas_call(
        functools.partial(_sb_prompt_kernel, tq=tq),
        grid=(B, SB_HEADS, nq),
        in_specs=[
            pl.BlockSpec((tq, SB_HEAD_DIM), lambda b, h, i: (b * nq + i, h)),
            pl.BlockSpec((L * SB_HEADS, SB_HEAD_DIM), lambda b, h, i: (b, 0)),
            pl.BlockSpec((L * SB_HEADS, SB_HEAD_DIM), lambda b, h, i: (b, 0)),
        ],
        out_specs=pl.BlockSpec((tq, SB_HEAD_DIM), lambda b, h, i: (b * nq + i, h)),
        out_shape=jax.ShapeDtypeStruct((B * L, SB_HEADS * SB_HEAD_DIM), BF16),
        scratch_shapes=[pltpu.VMEM((tq, SB_HEAD_DIM), F32), pltpu.VMEM((tq, 1), F32)],
        compiler_params=_params("parallel", "parallel", "arbitrary"),
        name="sb_attention_prompt",
    )(q, k, v)


def _sb_decode_kernel(qbd_ref, kn_ref, vn_ref, pk_ref, pv_ref, o_ref, knew_ref, vnew_ref, acc_ref,
                      *, ls, npast):
    H, D = SB_HEADS, SB_HEAD_DIM
    W = H * ls
    qbd = qbd_ref[...]
    scale = D ** -0.5
    acc_ref[...] = jnp.zeros_like(acc_ref)

    def block(k_row, v_row, nb, masked, carry):
        zt = _dot(k_row, qbd) * scale
        sp = _softplus_fast(zt)
        if masked:
            j = lax.broadcasted_iota(jnp.int32, (nb, W), 0)
            t = jnp.bitwise_and(lax.broadcasted_iota(jnp.int32, (nb, W), 1), ls - 1)
            valid = j < t
            sp = jnp.where(valid, sp, 0.0)
        rs = lax.broadcasted_iota(jnp.int32, (nb, nb), 0)
        rj = lax.broadcasted_iota(jnp.int32, (nb, nb), 1)
        tri_t = jnp.where(rj >= rs, 1.0, 0.0).astype(BF16)
        hi, lo = _split2(sp)
        cs = _dot(tri_t, hi) + _dot(tri_t, lo)
        w = jnp.exp(zt - cs - carry)
        if masked:
            w = jnp.where(valid, w, 0.0)
        acc_ref[...] += _dot(w.T.astype(BF16), v_row)
        return carry + cs[0:1, :]

    def rows(ref, k0, n):
        return jnp.concatenate(
            [ref[pl.ds(k0 * H + h, n, stride=H), :] for h in range(H)], axis=1).astype(BF16)

    knew_ref[...] = jnp.zeros_like(knew_ref)
    vnew_ref[...] = jnp.zeros_like(vnew_ref)
    knew_ref[0:ls, :] = rows(kn_ref, 0, ls)
    vnew_ref[0:ls, :] = rows(vn_ref, 0, ls)
    carry = block(knew_ref[...], vnew_ref[...], LANES, True, jnp.zeros((1, W), F32))

    def body(it, carry):
        k0 = pl.multiple_of((npast - 1 - it) * SB_SUB, SB_SUB)
        return block(rows(pk_ref, k0, SB_SUB), rows(pv_ref, k0, SB_SUB), SB_SUB, False, carry)

    lax.fori_loop(0, npast, body, carry)
    for h in range(H):
        o_ref[:, h * D:(h + 1) * D] = acc_ref[h * ls:(h + 1) * ls, h * D:(h + 1) * D].astype(o_ref.dtype)


def sb_attention_decode(q, kn, vn, cache_k, cache_v, layer, B, ls):
    H, D = SB_HEADS, SB_HEAD_DIM
    depth, _, P = cache_k.shape[:3]
    W = H * ls
    assert P % SB_SUB == 0 and ls <= LANES and ls & (ls - 1) == 0 and ls % SUBLANES == 0
    qbd = jnp.einsum("bthd,hg->bhdgt", q.reshape(B, ls, H, D), jnp.eye(H, dtype=q.dtype))
    qbd = qbd.reshape(B, H * D, W)
    row = pl.BlockSpec((ls, H * D), lambda b: (b, 0))
    new = pl.BlockSpec((ls * H, D), lambda b: (b, 0))
    past = pl.BlockSpec((None, None, P * H, D), lambda b: (layer, b, 0, 0))
    return pl.pallas_call(
        functools.partial(_sb_decode_kernel, ls=ls, npast=P // SB_SUB),
        grid=(B,),
        in_specs=[pl.BlockSpec((None, H * D, W), lambda b: (b, 0, 0)), new, new, past, past],
        out_specs=row,
        out_shape=jax.ShapeDtypeStruct((B * ls, H * D), BF16),
        scratch_shapes=[pltpu.VMEM((LANES, H * D), BF16), pltpu.VMEM((LANES, H * D), BF16),
                        pltpu.VMEM((W, H * D), F32)],
        compiler_params=_params("parallel"),
        name="sb_attention_decode",
    )(qbd, kn, vn, cache_k.reshape(depth, B, P * H, D), cache_v.reshape(depth, B, P * H, D))


def _conv_taps(xp_ref, x3, halo, K, sl):
    xp_ref[:, SUBLANES - (K - 1):SUBLANES, :] = halo
    xp_ref[:, SUBLANES:SUBLANES + sl, :] = x3
    base = SUBLANES - (K - 1)
    return [xp_ref[:, base + k:base + k + sl, :] for k in range(K)]


def conv_halos(x, state, B, L, tile):
    C = x.shape[1]
    km1 = state.shape[1]
    nt = L // tile
    if nt == 1:
        return state[:, None]
    tails = x.reshape(B, nt, tile, C)[:, :-1, tile - km1:, :]
    return jnp.concatenate([state[:, None], tails], axis=1)


def _lru_kernel(lx_ref, lg_ref, halo_ref, h0_ref, cw_ref, cb_ref, wa_ref, ba_ref, wx_ref, bx_ref,
                lam_ref, o_ref, hl_ref, xp_ref, a_ref, u_ref, h_ref, hc_ref, *, tl, nl):
    i = pl.program_id(1)
    W = lx_ref.shape[1]

    @pl.when(i == 0)
    def _():
        hc_ref[...] = h0_ref[...]

    taps = _conv_taps(xp_ref, lx_ref[...][None], halo_ref[...][None], 4, tl)
    xc = cb_ref[...]
    for k in range(4):
        xc = xc + cw_ref[k:k + 1, :] * taps[k][0]

    bd = W // LRU_BLOCKS
    for n in range(LRU_BLOCKS):
        sl = slice(n * bd, (n + 1) * bd)
        xn = xc[:, sl]
        xb = xn.astype(BF16)
        r = jax.nn.sigmoid(_dot(xb, wa_ref[n].astype(BF16)) + ba_ref[:, sl])
        ig = jax.nn.sigmoid(_dot(xb, wx_ref[n].astype(BF16)) + bx_ref[:, sl])
        log_a = LRU_C * r * (-_softplus(-lam_ref[:, sl]))
        a = jnp.exp(log_a)
        u = jnp.sqrt(-jnp.tanh(log_a) * (a * a + 1.0)) * (ig * xn)
        a_ref[:, sl] = a
        u_ref[:, sl] = u

    def step(t, h):
        h = a_ref[pl.ds(t, 1), :] * h + u_ref[pl.ds(t, 1), :]
        h_ref[pl.ds(t, 1), :] = h
        return h

    h = lax.fori_loop(0, tl, step, hc_ref[...], unroll=8)
    hc_ref[...] = h
    o_ref[...] = (h_ref[...] * _gelu_tanh(lg_ref[...])).astype(o_ref.dtype)

    @pl.when(i == nl - 1)
    def _():
        hl_ref[...] = h


def lru_branch(lx, lg, state, h0, cw, cb, wa, ba, wx, bx, lam, B, L, tl=256):
    W = lx.shape[1]
    tl = min(tl, L)
    nl = L // tl
    halo = conv_halos(lx, state, B, L, tl)
    kern = functools.partial(_lru_kernel, tl=tl, nl=nl)
    vec = lambda: pl.BlockSpec((1, W), lambda b, i: (0, 0))
    blk = lambda: pl.BlockSpec(wa.shape, lambda b, i: (0, 0, 0))
    o, hl = pl.pallas_call(
        kern,
        grid=(B, nl),
        in_specs=[
            pl.BlockSpec((tl, W), lambda b, i: (b * nl + i, 0)),
            pl.BlockSpec((tl, W), lambda b, i: (b * nl + i, 0)),
            pl.BlockSpec((None, None, 3, W), lambda b, i: (b, i, 0, 0)),
            pl.BlockSpec((None, 1, W), lambda b, i: (b, 0, 0)),
            pl.BlockSpec((4, W), lambda b, i: (0, 0)),
            vec(), blk(), vec(), blk(), vec(), vec(),
        ],
        out_specs=[
            pl.BlockSpec((tl, W), lambda b, i: (b * nl + i, 0)),
            pl.BlockSpec((None, 1, W), lambda b, i: (b, 0, 0)),
        ],
        out_shape=[
            jax.ShapeDtypeStruct((B * L, W), BF16),
            jax.ShapeDtypeStruct((B, 1, W), F32),
        ],
        scratch_shapes=[
            pltpu.VMEM((1, tl + SUBLANES, W), F32),
            pltpu.VMEM((tl, W), F32),
            pltpu.VMEM((tl, W), F32),
            pltpu.VMEM((tl, W), F32),
            pltpu.VMEM((1, W), F32),
        ],
        compiler_params=_params("parallel", "arbitrary"),
        name="lru_branch",
    )(lx, lg, halo, h0.reshape(B, 1, W), cw, cb.reshape(1, W), wa, ba.reshape(1, W),
      wx, bx.reshape(1, W), lam.reshape(1, W))
    return o, hl.reshape(B, W)


def _ssd_kernel(z_ref, xbc_ref, dt_ref, halo_ref, h0_ref, cw_ref, cb_ref, dtb_ref, alog_ref,
                dexp_ref, nw_ref, o_ref, hl_ref, xp_ref, y_ref, st_ref, *, q, nc, n_valid):
    c = pl.program_id(1)
    inner = SSD_HEADS * SSD_HEAD_DIM
    gw = inner // SSD_GROUPS
    epg = SSD_HEADS // SSD_GROUPS
    n = SSD_STATE

    @pl.when(c == 0)
    def _():
        st_ref[...] = h0_ref[...]

    taps = _conv_taps(xp_ref, xbc_ref[...][None], halo_ref[...][None], 4, q)
    xc = cb_ref[...]
    for k in range(4):
        xc = xc + cw_ref[k:k + 1, :] * taps[k][0]
    xc = _silu(xc)
    xs = xc[:, :inner]
    bm = xc[:, inner:inner + SSD_GROUPS * n]
    cm = xc[:, inner + SSD_GROUPS * n:]

    rowi = lax.broadcasted_iota(jnp.int32, (q, LANES), 0)
    lanei = lax.broadcasted_iota(jnp.int32, (q, LANES), 1)
    dts = jnp.where(lanei < SSD_HEADS, _softplus(dt_ref[...] + dtb_ref[...]), 0.0)
    if n_valid < q:
        dts = jnp.where(rowi < n_valid, dts, 0.0)
    dta = dts * (-jnp.exp(alog_ref[...]))

    ti = lax.broadcasted_iota(jnp.int32, (q, q), 0)
    si = lax.broadcasted_iota(jnp.int32, (q, q), 1)
    causal = si <= ti
    lower = jnp.where(causal, 1.0, 0.0).astype(BF16)
    cum = sum(_dot(lower, p) for p in _split3(dta))
    cum_t = cum.T
    cum_last = cum[q - 1:q, :]
    exp_cum = jnp.exp(cum)
    decay_end = jnp.exp(cum_last - cum)

    eh = lax.broadcasted_iota(jnp.int32, (LANES, inner), 0)
    ec = lax.broadcasted_iota(jnp.int32, (LANES, inner), 1)
    expand = jnp.where(ec // SSD_HEAD_DIM == eh, 1.0, 0.0).astype(BF16)

    def expand_heads(a):
        return sum(_dot(p, expand) for p in _split3(a))

    dts_e = expand_heads(dts)
    exp_cum_e = expand_heads(exp_cum)
    decay_end_e = expand_heads(decay_end)
    chunk_decay_e = exp_cum_e[q - 1:q, :]

    xdt = xs * dts_e
    xdt_b = xdt.astype(BF16)
    xend_b = (decay_end_e * xdt).astype(BF16)

    for g in range(SSD_GROUPS):
        bg = bm[:, g * n:(g + 1) * n]
        cg_b = cm[:, g * n:(g + 1) * n].astype(BF16)
        gs = slice(g * gw, (g + 1) * gw)
        cb = _dot_nt(cg_b, bg.astype(BF16))
        st = st_ref[g]
        y_inter = _dot(cg_b, st.astype(BF16)) * exp_cum_e[:, gs]
        for e in range(epg):
            h = g * epg + e
            hs = slice(h * SSD_HEAD_DIM, (h + 1) * SSD_HEAD_DIM)
            seg = cum[:, h:h + 1] - cum_t[h:h + 1, :]
            decay = jnp.where(causal, jnp.exp(seg), 0.0)
            wts = (cb * decay).astype(BF16)
            y_ref[:, hs] = _dot(wts, xdt_b[:, hs]) + y_inter[:, e * SSD_HEAD_DIM:(e + 1) * SSD_HEAD_DIM]
        bg_t = bg.T.astype(BF16)
        st_ref[g] = chunk_decay_e[:, gs] * st + _dot(bg_t, xend_b[:, gs])

    y = y_ref[...] + dexp_ref[...] * xs
    yg = y * _silu(z_ref[...])
    for g in range(SSD_GROUPS):
        gs = slice(g * gw, (g + 1) * gw)
        part = yg[:, gs]
        ms = jnp.mean(part * part, axis=-1, keepdims=True)
        o_ref[:, gs] = (part * lax.rsqrt(ms + NORM_EPS) * nw_ref[:, gs]).astype(o_ref.dtype)

    @pl.when(c == nc - 1)
    def _():
        hl_ref[...] = st_ref[...]


def _pad_rows(x, B, L, Lp):
    if Lp == L:
        return x
    C = x.shape[1]
    return jnp.pad(x.reshape(B, L, C), ((0, 0), (0, Lp - L), (0, 0))).reshape(B * Lp, C)


def ssd_branch(z, xbc, dt, dt_block, state, h0, cw, cb, dt_bias, a_log, d, norm_w, B, L, q=128):
    inner = z.shape[1]
    cd = xbc.shape[1]
    nh = SSD_HEADS
    Lp = max(L, q) if L % q else L
    n_valid = min(L, q)
    nc = Lp // q
    gw = inner // SSD_GROUPS
    epg = SSD_HEADS // SSD_GROUPS
    halo = conv_halos(xbc, state, B, L, q) if L >= q else state[:, None]
    zp = _pad_rows(z, B, L, Lp)
    xbcp = _pad_rows(xbc, B, L, Lp)
    if Lp != L:
        dt, dt_block = dt[:, dt_block * LANES:(dt_block + 1) * LANES], 0
    dtp = _pad_rows(dt, B, L, Lp)
    h0t = h0.reshape(B, SSD_GROUPS, epg, SSD_HEAD_DIM, SSD_STATE).transpose(0, 1, 4, 2, 3)
    h0t = h0t.reshape(B, SSD_GROUPS, SSD_STATE, gw)
    pad1 = lambda a: jnp.pad(a, (0, LANES - nh)).reshape(1, LANES)
    kern = functools.partial(_ssd_kernel, q=q, nc=nc, n_valid=n_valid)
    row = lambda w: pl.BlockSpec((q, w), lambda b, c: (b * nc + c, 0))
    vec = lambda w: pl.BlockSpec((1, w), lambda b, c: (0, 0))
    o, hl = pl.pallas_call(
        kern,
        grid=(B, nc),
        in_specs=[
            row(inner), row(cd), pl.BlockSpec((q, LANES), lambda b, c: (b * nc + c, dt_block)),
            pl.BlockSpec((None, None, 3, cd), lambda b, c: (b, c, 0, 0)),
            pl.BlockSpec((None, SSD_GROUPS, SSD_STATE, gw), lambda b, c: (b, 0, 0, 0)),
            pl.BlockSpec((4, cd), lambda b, c: (0, 0)),
            vec(cd), vec(LANES), vec(LANES), vec(inner), vec(inner),
        ],
        out_specs=[
            row(inner),
            pl.BlockSpec((None, SSD_GROUPS, SSD_STATE, gw), lambda b, c: (b, 0, 0, 0)),
        ],
        out_shape=[
            jax.ShapeDtypeStruct((B * Lp, inner), BF16),
            jax.ShapeDtypeStruct((B, SSD_GROUPS, SSD_STATE, gw), F32),
        ],
        scratch_shapes=[
            pltpu.VMEM((1, q + SUBLANES, cd), F32),
            pltpu.VMEM((q, inner), F32),
            pltpu.VMEM((SSD_GROUPS, SSD_STATE, gw), F32),
        ],
        compiler_params=_params("parallel", "arbitrary"),
        name="ssd_branch",
    )(zp, xbcp, dtp, halo, h0t, cw, cb.reshape(1, cd), pad1(dt_bias), pad1(a_log),
      jnp.repeat(d, SSD_HEAD_DIM).reshape(1, inner), norm_w.reshape(1, inner))
    if Lp != L:
        o = o.reshape(B, Lp, inner)[:, :L].reshape(B * L, inner)
    hl = hl.reshape(B, SSD_GROUPS, SSD_STATE, epg, SSD_HEAD_DIM).transpose(0, 1, 3, 4, 2)
    return o, hl.reshape(B, SSD_HEADS, SSD_HEAD_DIM, SSD_STATE)


def _merge_kernel(x_ref, g_ref, a_ref, b_ref, c_ref, wg0_ref, wg1_ref, wg2_ref, wb_ref, o_ref, xn_ref):
    @pl.when(pl.program_id(1) == 0)
    def _():
        x = x_ref[...]
        ms = jnp.mean(x * x, axis=-1, keepdims=True)
        xn_ref[...] = (x * lax.rsqrt(ms + NORM_EPS) * g_ref[...]).astype(BF16)

    xn = xn_ref[...]
    acc = None
    for n, (br, wg) in enumerate(((a_ref, wg0_ref), (b_ref, wg1_ref), (c_ref, wg2_ref))):
        gate = _dot_nt(xn, wg[...].astype(BF16))
        t = _dot(br[...], wb_ref[n].astype(BF16)) * jax.nn.sigmoid(gate)
        acc = t if acc is None else acc + t
    o_ref[...] = acc.astype(o_ref.dtype)


def gated_merge(x, g, oa, ob, oc, w_gate_t, wb, layer, tm=1024, tn=256):
    T, K = oa.shape
    D = x.shape[1]
    N = wb.shape[-1]
    tm = min(tm, T)
    nj = N // tn
    br = lambda: pl.BlockSpec((tm, K), lambda i, j: (i, 0))
    gate = lambda n: pl.BlockSpec((None, tn, D), lambda i, j: (layer, n * nj + j, 0))
    return pl.pallas_call(
        _merge_kernel,
        grid=(T // tm, nj),
        in_specs=[pl.BlockSpec((tm, D), lambda i, j: (i, 0), pipeline_mode=pl.Buffered(1)),
                  pl.BlockSpec((1, D), lambda i, j: (0, 0)),
                  br(), br(), br(), gate(0), gate(1), gate(2),
                  pl.BlockSpec((None, 3, K, tn), lambda i, j: (layer, 0, 0, j))],
        out_specs=pl.BlockSpec((tm, tn), lambda i, j: (i, j)),
        out_shape=jax.ShapeDtypeStruct((T, N), BF16),
        scratch_shapes=[pltpu.VMEM((tm, D), BF16)],
        compiler_params=_params("parallel", "arbitrary"),
        name="gated_merge",
    )(x, g.reshape(1, D), oa, ob, oc, w_gate_t, w_gate_t, w_gate_t, wb)


def _xattn_kernel(q_ref, mk_ref, mv_ref, o_ref):
    hd = q_ref.shape[1] // XA_HEADS
    scale = hd ** -0.5
    for h in range(XA_HEADS):
        sl = slice(h * hd, (h + 1) * hd)
        kh, vh = (mk_ref[:, sl], mv_ref[:, sl]) if len(mk_ref.shape) == 2 else (mk_ref[:, h, :], mv_ref[:, h, :])
        s = _dot_nt(q_ref[:, sl].astype(BF16), kh.astype(BF16)) * scale
        m = jnp.max(s, axis=-1, keepdims=True)
        p = jnp.exp(s - m)
        p = p / jnp.sum(p, axis=-1, keepdims=True)
        o_ref[:, sl] = _dot(p.astype(BF16), vh.astype(BF16)).astype(o_ref.dtype)


def cross_attention(q, mk, mv, layer, B, L, tq=512):
    D = q.shape[1]
    tq = min(tq, L)
    nq = L // tq
    if layer is None:
        mem = lambda: pl.BlockSpec((None,) + mk.shape[1:], lambda b, i: (b, 0, 0))
    else:
        mem = lambda: pl.BlockSpec((None, None) + mk.shape[2:], lambda b, i: (layer, b, 0, 0, 0))
    return pl.pallas_call(
        _xattn_kernel,
        grid=(B, nq),
        in_specs=[pl.BlockSpec((tq, D), lambda b, i: (b * nq + i, 0)), mem(), mem()],
        out_specs=pl.BlockSpec((tq, D), lambda b, i: (b * nq + i, 0)),
        out_shape=jax.ShapeDtypeStruct((B * L, D), BF16),
        compiler_params=_params("parallel", "arbitrary"),
        name="cross_attention",
    )(q, mk, mv)


def _mem_kv_kernel(x_ref, g_ref, wk_ref, wv_ref, k_ref, v_ref, k2_ref, v2_ref, xn_ref):
    j = pl.program_id(1)

    @pl.when(j == 0)
    def _():
        x = x_ref[...]
        ms = jnp.mean(x * x, axis=-1, keepdims=True)
        xn_ref[...] = (x * lax.rsqrt(ms + NORM_EPS) * g_ref[...]).astype(BF16)

    xn = xn_ref[...]
    k = _dot(xn, wk_ref[...].astype(BF16))
    v = _dot(xn, wv_ref[...].astype(BF16))
    k2_ref[...] = k.astype(k2_ref.dtype)
    v2_ref[...] = v.astype(v2_ref.dtype)
    for h in range(XA_HEADS):
        @pl.when(j == h)
        def _(h=h):
            k_ref[:, h, :] = k
            v_ref[:, h, :] = v


def mem_kv(mem, g, wk, wv, layer, B, M):
    D = mem.shape[1]
    hd = D // XA_HEADS
    w = lambda: pl.BlockSpec((None, D, hd), lambda b, j: (layer, 0, j))
    o4 = lambda: pl.BlockSpec((None, M, XA_HEADS, hd), lambda b, j: (b, 0, 0, 0))
    o2 = lambda: pl.BlockSpec((None, M, hd), lambda b, j: (b, 0, j))
    return pl.pallas_call(
        _mem_kv_kernel,
        grid=(B, XA_HEADS),
        in_specs=[pl.BlockSpec((M, D), lambda b, j: (b, 0)), pl.BlockSpec((1, D), lambda b, j: (0, 0)),
                  w(), w()],
        out_specs=[o4(), o4(), o2(), o2()],
        out_shape=[jax.ShapeDtypeStruct((B, M, XA_HEADS, hd), F32)] * 2
        + [jax.ShapeDtypeStruct((B, M, D), BF16)] * 2,
        scratch_shapes=[pltpu.VMEM((M, D), BF16)],
        compiler_params=_params("parallel", "arbitrary"),
        name="mem_kv",
    )(mem, g.reshape(1, D), wk, wv)


def _ffn_up_kernel(x_ref, g_ref, wg_ref, wv_ref, sg_ref, sv_ref, cwg_ref, cwv_ref, cbg_ref, cbv_ref,
                   act_ref, tg_ref, tv_ref, xn_ref, xp_ref, carry_ref, *, nseg, sl, tpb):
    i = pl.program_id(0)
    f = pl.program_id(1)
    tm, tf = act_ref.shape

    @pl.when(f == 0)
    def _():
        x = x_ref[...]
        ms = jnp.mean(x * x, axis=-1, keepdims=True)
        xn_ref[...] = (x * lax.rsqrt(ms + NORM_EPS) * g_ref[...]).astype(BF16)

    if tpb > 1:
        @pl.when((i == 0) & (f == 0))
        def _():
            carry_ref[...] = jnp.zeros_like(carry_ref)

    xn = xn_ref[...]
    base = SUBLANES - 2
    starts = i % tpb == 0
    ts = xp_ref.shape[-1]

    def half(s, c, w_ref, st_ref, cw_ref, cb_ref, tail_ref):
        cols = slice(s * ts, (s + 1) * ts)
        xp = xp_ref.at[2 * s + c]
        u3 = _dot(xn, w_ref[:, cols].astype(BF16)).reshape(nseg, sl, ts)
        halo = st_ref[:, :, cols]
        if tpb > 1:
            halo = jnp.where(starts, halo, carry_ref[c, f, :, :, cols])
        xp[:, base:SUBLANES, :] = halo
        xp[:, SUBLANES:SUBLANES + sl, :] = u3
        y = cb_ref[:, cols]
        for k in range(3):
            y = y + cw_ref[k:k + 1, cols] * xp[:, base + k:base + k + sl, :].reshape(tm, ts)
        tail = xp[:, SUBLANES + sl - 2:SUBLANES + sl, :]
        if tpb > 1:
            carry_ref[c, f, :, :, cols] = tail
        tail_ref[:, :, cols] = tail
        return y

    for s in range(tf // ts):
        gate = half(s, 0, wg_ref, sg_ref, cwg_ref, cbg_ref, tg_ref)
        val = half(s, 1, wv_ref, sv_ref, cwv_ref, cbv_ref, tv_ref)
        act_ref[:, s * ts:(s + 1) * ts] = (_silu(gate) * val).astype(act_ref.dtype)


def ffn_up(x, g, w_up, layer, state, cw, cb, B, L, tm=1024, tf=512):
    T, D = x.shape
    F2 = w_up.shape[-1]
    F = F2 // 2
    if L >= tm:
        nseg, sl, tpb = 1, tm, L // tm
    else:
        nseg, sl, tpb = min(tm // L, B), L, 1
        tm = nseg * sl
    nt, nf = T // tm, F // tf
    ts = min(tf, MXU_WIDTH)
    st = state.reshape(B // nseg, nseg, 2, F2)
    cb2 = cb.reshape(1, F2)
    kern = functools.partial(_ffn_up_kernel, nseg=nseg, sl=sl, tpb=tpb)
    wspec = lambda off: pl.BlockSpec((None, D, tf), lambda i, f: (layer, 0, off + f))
    sspec = lambda off: pl.BlockSpec((None, nseg, 2, tf), lambda i, f: (i // tpb, 0, 0, off + f))
    cwspec = lambda off: pl.BlockSpec((3, tf), lambda i, f: (0, off + f))
    cbspec = lambda off: pl.BlockSpec((1, tf), lambda i, f: (0, off + f))
    tspec = lambda: pl.BlockSpec((None, nseg, 2, tf), lambda i, f: (i, 0, 0, f))
    act, tg, tv = pl.pallas_call(
        kern,
        grid=(nt, nf),
        in_specs=[
            pl.BlockSpec((tm, D), lambda i, f: (i, 0), pipeline_mode=pl.Buffered(1)),
            pl.BlockSpec((1, D), lambda i, f: (0, 0)),
            wspec(0), wspec(nf), sspec(0), sspec(nf), cwspec(0), cwspec(nf), cbspec(0), cbspec(nf),
        ],
        out_specs=[pl.BlockSpec((tm, tf), lambda i, f: (i, f)), tspec(), tspec()],
        out_shape=[
            jax.ShapeDtypeStruct((T, F), BF16),
            jax.ShapeDtypeStruct((nt, nseg, 2, F), F32),
            jax.ShapeDtypeStruct((nt, nseg, 2, F), F32),
        ],
        scratch_shapes=[
            pltpu.VMEM((tm, D), BF16),
            pltpu.VMEM((2 * (tf // ts), nseg, sl + SUBLANES, ts), F32),
            pltpu.VMEM((2, nf, nseg, 2, tf), F32),
        ],
        compiler_params=_params("arbitrary", "arbitrary"),
        name="ffn_up",
    )(x, g.reshape(1, D), w_up, w_up, st, st, cw, cw, cb2, cb2)
    last = lambda t: t.reshape(B // nseg, tpb, nseg, 2, F)[:, -1].reshape(B, 2, F)
    return act, jnp.concatenate([last(tg), last(tv)], axis=-1)


def _last_rows(x, state, B, L):
    km1 = state.shape[1]
    x3 = x.reshape(B, L, x.shape[1])
    if L >= km1:
        return x3[:, L - km1:]
    return jnp.concatenate([state, x3], axis=1)[:, -km1:]


def trunk_layer(x, B, L, mk, mv, mem_layer, past_k, past_v, lru_buf, lru_h, ssd_buf, ssd_h, ffn_buf, w, l):
    mix = SB_HEADS * SB_HEAD_DIM
    inner = SSD_HEADS * SSD_HEAD_DIM
    cd = inner + 2 * SSD_GROUPS * SSD_STATE
    D = x.shape[1]
    tn = 512
    q, k, v, lx, lg, z, xbc, dtx = in_proj(x, w["norm_mix"][l], w["w_in_t"], l,
                                           (mix,) * 5 + (inner, cd, tn), (BF16,) + (F32,) * 7,
                                           (False, True, True) + (False,) * 5, SB_HEAD_DIM, tn=tn)

    if past_k is None:
        o_a = sb_attention_prompt(q, k, v, B, L)
    else:
        o_a = sb_attention_decode(q, k, v, past_k, past_v, l, B, L)
    o_b, lru_h_new = lru_branch(lx, lg, lru_buf, lru_h, w["lru_conv_w"][l], w["lru_conv_b"][l], w["lru_wa"][l],
                                w["lru_ba"][l], w["lru_wx"][l], w["lru_bx"][l], w["lru_lam"][l], B, L)
    o_c, ssd_h_new = ssd_branch(z, xbc, dtx, 0, ssd_buf, ssd_h, w["ssd_conv_w"][l],
                                w["ssd_conv_b"][l], w["ssd_dt_bias"][l], w["ssd_a_log"][l], w["ssd_d"][l],
                                w["ssd_norm"][l], B, L)
    merged = gated_merge(x, w["norm_mix"][l], o_a, o_b, o_c, w["w_gate_t"], w["w_branch"], l)
    h = matmul_res(merged, w["w_out"], l, x, tm=2048)
    qx = norm_matmul(h, w["norm_xattn"][l], w["w_xq"], l, out_dtype=BF16)
    o_x = cross_attention(qx, mk, mv, mem_layer, B, L)
    h = matmul_res(o_x, w["w_xo"], l, h, tm=2048)
    act, ffn_new = ffn_up(h, w["norm_ffn"][l], w["w_up"], l, ffn_buf, w["ffn_conv_w"][l],
                          w["ffn_conv_b"][l], B, L)
    out = matmul_res(act, w["w_down"], l, h, tn=256)
    states = (k.reshape(B, L, SB_HEADS, SB_HEAD_DIM), v.reshape(B, L, SB_HEADS, SB_HEAD_DIM),
              _last_rows(lx, lru_buf, B, L), lru_h_new, _last_rows(xbc, ssd_buf, B, L), ssd_h_new,
              ffn_new)
    return out, states


def kernel(x_prompt, x_sample, mem_prompt, cache_sb_k, cache_sb_v, cache_mem_k, cache_mem_v, state_lru_conv, state_lru_h, state_ssd_conv, state_ssd, state_ffn_conv, norm_mix, w_in, lru_conv_w, lru_conv_b, lru_wa, lru_ba, lru_wx, lru_bx, lru_lam, ssd_conv_w, ssd_conv_b, ssd_dt_bias, ssd_a_log, ssd_d, ssd_norm, w_branch, w_out, norm_xattn, norm_mem, w_xq, w_xk, w_xv, w_xo, norm_ffn, w_up, ffn_conv_w, ffn_conv_b, w_down, norm_final):
    depth = w_in.shape[0]
    Bp, Lp, D = x_prompt.shape
    Bs, Ls, _ = x_sample.shape
    M = mem_prompt.shape[1]
    mix = SB_HEADS * SB_HEAD_DIM
    inner = SSD_HEADS * SSD_HEAD_DIM
    cd = inner + 2 * SSD_GROUPS * SSD_STATE
    f2 = w_up.shape[2]
    xa_hd = D // XA_HEADS

    yp = x_prompt.reshape(Bp * Lp, D)
    ys = x_sample.reshape(Bs * Ls, D)
    mem = mem_prompt.reshape(Bp * M, D)
    st_p, st_s, mk_list, mv_list = [], [], [], []
    w_in_t = jnp.swapaxes(w_in, 1, 2)
    w_gate_t = w_in_t[:, 5 * mix + inner + cd + SSD_HEADS:]
    w = dict(w_in_t=w_in_t, w_gate_t=w_gate_t, norm_mix=norm_mix, lru_conv_w=lru_conv_w, lru_conv_b=lru_conv_b,
             lru_wa=lru_wa, lru_ba=lru_ba, lru_wx=lru_wx, lru_bx=lru_bx, lru_lam=lru_lam,
             ssd_conv_w=ssd_conv_w, ssd_conv_b=ssd_conv_b, ssd_dt_bias=ssd_dt_bias, ssd_a_log=ssd_a_log,
             ssd_d=ssd_d, ssd_norm=ssd_norm, w_branch=w_branch, w_out=w_out, norm_xattn=norm_xattn,
             w_xq=w_xq, w_xo=w_xo, norm_ffn=norm_ffn, w_up=w_up, ffn_conv_w=ffn_conv_w,
             ffn_conv_b=ffn_conv_b, w_down=w_down)
    for l in range(depth):
        mk_p, mv_p, mk2, mv2 = mem_kv(mem, norm_mem[l], w_xk, w_xv, l, Bp, M)
        zeros = lambda *s: jnp.zeros(s, F32)
        yp, sp = trunk_layer(yp, Bp, Lp, mk2, mv2, None, None, None, zeros(Bp, 3, mix),
                             zeros(Bp, mix), zeros(Bp, 3, cd), zeros(Bp, SSD_HEADS, SSD_HEAD_DIM, SSD_STATE),
                             zeros(Bp, 2, f2), w, l)
        ys, ss = trunk_layer(ys, Bs, Ls, cache_mem_k, cache_mem_v, l,
                             cache_sb_k, cache_sb_v, state_lru_conv[l], state_lru_h[l],
                             state_ssd_conv[l], state_ssd[l], state_ffn_conv[l], w, l)
        st_p.append(sp)
        st_s.append(ss)
        mk_list.append(mk_p)
        mv_list.append(mv_p)
    y_prompt = rmsnorm(yp, norm_final).reshape(Bp, Lp, D)
    y_sample = rmsnorm(ys, norm_final).reshape(Bs, Ls, D)

    def stk(lst, i):
        return jnp.stack([s[i] for s in lst], axis=0)

    return (y_prompt, y_sample,
            stk(st_p, 0), stk(st_p, 1), stk(st_p, 2), stk(st_p, 3), stk(st_p, 4), stk(st_p, 5), stk(st_p, 6),
            jnp.stack(mk_list, axis=0), jnp.stack(mv_list, axis=0),
            stk(st_s, 0), stk(st_s, 1), stk(st_s, 2), stk(st_s, 3), stk(st_s, 4), stk(st_s, 5), stk(st_s, 6))
```

```python
import functools
import math

import jax
import jax.numpy as jnp
from jax import lax
from jax.experimental import pallas as pl
from jax.experimental.pallas import tpu as pltpu

F32 = jnp.float32
BF16 = jnp.bfloat16

NORM_EPS = 1e-6
LRU_C = 8.0
LANES = 128
SUBLANES = 8
MXU_WIDTH = 256
CHUNK_ROWS = 32
VMEM_LIMIT_BYTES = 56 * 1024 * 1024

SB_HEADS = 8
SB_HEAD_DIM = 128
LRU_BLOCKS = 8
SSD_HEADS = 16
SSD_HEAD_DIM = 64
SSD_GROUPS = 2
SSD_STATE = 128
XA_HEADS = 4


def _params(*sem):
    return pltpu.CompilerParams(dimension_semantics=sem, vmem_limit_bytes=VMEM_LIMIT_BYTES)


def _dot(a, b):
    return jnp.dot(a, b, preferred_element_type=F32)


def _dot_nt(a, b):
    return lax.dot_general(a, b, (((1,), (1,)), ((), ())), preferred_element_type=F32)


def _split3(x):
    hi = x.astype(BF16)
    r = x - hi.astype(F32)
    mid = r.astype(BF16)
    lo = (r - mid.astype(F32)).astype(BF16)
    return hi, mid, lo


def _softplus(x):
    return jnp.maximum(x, 0.0) + jnp.log1p(jnp.exp(-jnp.abs(x)))


def _silu(x):
    return x * jax.nn.sigmoid(x)


def _gelu_tanh(x):
    c = math.sqrt(2.0 / math.pi)
    return 0.5 * x * (1.0 + jnp.tanh(c * (x + 0.044715 * (x * x * x))))


def _norm_matmul_kernel(x_ref, g_ref, w_ref, o_ref, xn_ref, *, transposed):
    @pl.when(pl.program_id(1) == 0)
    def _():
        x = x_ref[...]
        ms = jnp.mean(x * x, axis=-1, keepdims=True)
        xn_ref[...] = (x * lax.rsqrt(ms + NORM_EPS) * g_ref[...]).astype(BF16)

    w = w_ref[...].astype(BF16)
    acc = _dot_nt(xn_ref[...], w) if transposed else _dot(xn_ref[...], w)
    o_ref[...] = acc.astype(o_ref.dtype)


def _layer_spec(w, layer, block, index):
    if layer is None:
        return pl.BlockSpec(block, index)
    return pl.BlockSpec((None,) + block, lambda *ids: (layer,) + index(*ids))


def norm_matmul(x, g, w, layer=None, out_dtype=F32, transposed=False, tm=1024, tn=512):
    T, D = x.shape
    N = w.shape[-2] if transposed else w.shape[-1]
    tm = min(tm, T)
    tn = min(tn, N)
    w_spec = (_layer_spec(w, layer, (tn, D), lambda i, j: (j, 0)) if transposed
              else _layer_spec(w, layer, (D, tn), lambda i, j: (0, j)))
    return pl.pallas_call(
        functools.partial(_norm_matmul_kernel, transposed=transposed),
        grid=(T // tm, N // tn),
        in_specs=[
            pl.BlockSpec((tm, D), lambda i, j: (i, 0)),
            pl.BlockSpec((1, D), lambda i, j: (0, 0)),
            w_spec,
        ],
        out_specs=pl.BlockSpec((tm, tn), lambda i, j: (i, j)),
        out_shape=jax.ShapeDtypeStruct((T, N), out_dtype),
        scratch_shapes=[pltpu.VMEM((tm, D), BF16)],
        compiler_params=_params("parallel", "arbitrary"),
        name="norm_matmul",
    )(x, g.reshape(1, D), w)


def _in_proj_kernel(x_ref, g_ref, w_ref, *refs, ranges, by_head, hd):
    out_refs, xn_ref = refs[:-1], refs[-1]
    j = pl.program_id(1)
    tm, tn = xn_ref.shape[0], w_ref.shape[0]

    @pl.when(j == 0)
    def _():
        x = x_ref[...]
        ms = jnp.mean(x * x, axis=-1, keepdims=True)
        xn_ref[...] = (x * lax.rsqrt(ms + NORM_EPS) * g_ref[...]).astype(BF16)

    acc = _dot_nt(xn_ref[...], w_ref[...].astype(BF16))
    for o_ref, (lo, hi), heads in zip(out_refs, ranges, by_head):
        @pl.when((j >= lo) & (j < hi))
        def _(o_ref=o_ref, lo=lo, heads=heads):
            if heads:
                nh = (hi - lo) * (tn // hd)
                for c in range(tn // hd):
                    o_ref[pl.ds((j - lo) * (tn // hd) + c, tm, stride=nh), :] = acc[:, c * hd:(c + 1) * hd]
            else:
                o_ref[...] = acc.astype(o_ref.dtype)


def in_proj(x, g, w_t, layer, widths, dtypes, by_head, hd, tm=1024, tn=512):
    T, D = x.shape
    tm = min(tm, T)
    ranges, nblocks = [], 0
    for wd in widths:
        assert wd % tn == 0
        ranges.append((nblocks, nblocks + wd // tn))
        nblocks += wd // tn
    out_specs, out_shape = [], []
    for (lo, hi), wd, dt, heads in zip(ranges, widths, dtypes, by_head):
        if heads:
            out_specs.append(pl.BlockSpec((tm * (wd // hd), hd), lambda i, j: (i, 0),
                                          pipeline_mode=pl.Buffered(1)))
            out_shape.append(jax.ShapeDtypeStruct((T * (wd // hd), hd), dt))
        else:
            out_specs.append(pl.BlockSpec(
                (tm, tn), lambda i, j, lo=lo, n=hi - lo: (i, jnp.clip(j - lo, 0, n - 1))))
            out_shape.append(jax.ShapeDtypeStruct((T, wd), dt))
    return pl.pallas_call(
        functools.partial(_in_proj_kernel, ranges=tuple(ranges), by_head=tuple(by_head), hd=hd),
        grid=(T // tm, nblocks),
        in_specs=[
            pl.BlockSpec((tm, D), lambda i, j: (i, 0), pipeline_mode=pl.Buffered(1)),
            pl.BlockSpec((1, D), lambda i, j: (0, 0)),
            pl.BlockSpec((None, tn, D), lambda i, j: (layer, j, 0)),
        ],
        out_specs=out_specs,
        out_shape=out_shape,
        scratch_shapes=[pltpu.VMEM((tm, D), BF16)],
        compiler_params=_params("parallel", "arbitrary"),
        name="in_proj",
    )(x, g.reshape(1, D), w_t)


def _matmul_res_kernel(x_ref, w_ref, r_ref, o_ref):
    o_ref[...] = r_ref[...] + _dot(x_ref[...], w_ref[...].astype(BF16))


def matmul_res(x, w, layer, res, tm=1024, tn=512):
    T, K = x.shape
    N = w.shape[-1]
    tm = min(tm, T)
    tn = min(tn, N)
    return pl.pallas_call(
        _matmul_res_kernel,
        grid=(T // tm, N // tn),
        in_specs=[
            pl.BlockSpec((tm, K), lambda i, j: (i, 0)),
            _layer_spec(w, layer, (K, tn), lambda i, j: (0, j)),
            pl.BlockSpec((tm, tn), lambda i, j: (i, j)),
        ],
        out_specs=pl.BlockSpec((tm, tn), lambda i, j: (i, j)),
        out_shape=jax.ShapeDtypeStruct((T, N), F32),
        compiler_params=_params("parallel", "arbitrary"),
        name="matmul_res",
    )(x, w, res)


def _rmsnorm_kernel(x_ref, g_ref, o_ref):
    x = x_ref[...]
    ms = jnp.mean(x * x, axis=-1, keepdims=True)
    o_ref[...] = x * lax.rsqrt(ms + NORM_EPS) * g_ref[...]


def rmsnorm(x, g, tm=512):
    T, D = x.shape
    tm = min(tm, T)
    return pl.pallas_call(
        _rmsnorm_kernel,
        grid=(T // tm,),
        in_specs=[pl.BlockSpec((tm, D), lambda i: (i, 0)), pl.BlockSpec((1, D), lambda i: (0, 0))],
        out_specs=pl.BlockSpec((tm, D), lambda i: (i, 0)),
        out_shape=jax.ShapeDtypeStruct((T, D), F32),
        compiler_params=_params("parallel"),
        name="rmsnorm",
    )(x, g.reshape(1, D))


SB_SUB = MXU_WIDTH


def _softplus_fast(z):
    return jnp.maximum(z, 0.0) + jnp.log(1.0 + jnp.exp(-jnp.abs(z)))


def _split2(x):
    hi = x.astype(BF16)
    return hi, (x - hi.astype(F32)).astype(BF16)


def _sb_prompt_kernel(q_ref, k_ref, v_ref, o_ref, acc_ref, carry_ref, z_ref, cs_ref, hi_ref, lo_ref, w_ref,
                      *, tq, hg):
    h0 = pl.program_id(1) * hg
    qi = pl.program_id(2)
    D = SB_HEAD_DIM
    rj = lax.broadcasted_iota(jnp.int32, (SB_SUB, SB_SUB), 0)
    rs = lax.broadcasted_iota(jnp.int32, (SB_SUB, SB_SUB), 1)
    tri = jnp.where(rj >= rs, 1.0, 0.0).astype(BF16)
    scale = D ** -0.5
    acc_ref[...] = jnp.zeros_like(acc_ref)
    carry_ref[...] = jnp.zeros_like(carry_ref)

    def chunk(c0, masked):
        c0 = pl.multiple_of(c0, tq)
        rc = CHUNK_ROWS

        def valid(r0):
            row = r0 + lax.broadcasted_iota(jnp.int32, (rc, tq), 0)
            return lax.broadcasted_iota(jnp.int32, (rc, tq), 1) < row

        def rows(ref, g):
            return ref[pl.ds(c0 * SB_HEADS + h0 + g, tq, stride=SB_HEADS), :].astype(BF16)

        def stages(g):
            def scores():
                z_ref[g] = _dot_nt(q_ref[:, g * D:(g + 1) * D], rows(k_ref, g)) * scale

            def split(r0):
                sp = _softplus_fast(z_ref[g, r0:r0 + rc, :])
                if masked:
                    sp = jnp.where(valid(r0), sp, 0.0)
                hi, lo = _split2(sp)
                hi_ref[g, r0:r0 + rc, :] = hi
                lo_ref[g, r0:r0 + rc, :] = lo

            def suffix(s):
                sl = slice(s * SB_SUB, (s + 1) * SB_SUB)
                cs = _dot(hi_ref[g, :, sl], tri) + _dot(lo_ref[g, :, sl], tri)
                carry = carry_ref[g]
                cs_ref[g, :, sl] = cs + carry
                carry_ref[g] = carry + cs[:, 0:1]

            def weights(r0):
                w = jnp.exp(z_ref[g, r0:r0 + rc, :] - cs_ref[g, r0:r0 + rc, :])
                if masked:
                    w = jnp.where(valid(r0), w, 0.0)
                w_ref[g, r0:r0 + rc, :] = w.astype(BF16)

            def values():
                acc_ref[g] += _dot(w_ref[g], rows(v_ref, g))

            part = functools.partial
            return [[scores],
                    [part(split, r0) for r0 in range(0, tq, rc)],
                    [part(suffix, s) for s in reversed(range(tq // SB_SUB))],
                    [part(weights, r0) for r0 in range(0, tq, rc)],
                    [values]]

        def emit(*lists):
            n = max(len(lst) for lst in lists)
            for t in range(n):
                for lst in lists:
                    for f in lst[t * len(lst) // n:(t + 1) * len(lst) // n]:
                        f()

        per_head = [stages(g) for g in range(hg)]
        for t in range(5 + hg - 1):
            emit(*[per_head[g][t - g] for g in range(hg) if 0 <= t - g < 5])

    chunk(qi * tq, True)

    def body(it, _):
        chunk((qi - 1 - it) * tq, False)
        return 0

    lax.fori_loop(0, qi, body, 0)
    for g in range(hg):
        o_ref[:, g * D:(g + 1) * D] = acc_ref[g].astype(o_ref.dtype)


def sb_attention_prompt(q, k, v, B, L, tq=512, hg=4):
    assert L % tq == 0 and tq % SB_SUB == 0 and SB_HEADS % hg == 0 and tq % CHUNK_ROWS == 0
    nq = L // tq
    kv = lambda: pl.BlockSpec((L * SB_HEADS, SB_HEAD_DIM), lambda b, h, i: (b, 0),
                              pipeline_mode=pl.Buffered(1))
    return pl.pallas_call(
        functools.partial(_sb_prompt_kernel, tq=tq, hg=hg),
        grid=(B, SB_HEADS // hg, nq),
        in_specs=[pl.BlockSpec((tq, hg * SB_HEAD_DIM), lambda b, h, i: (b * nq + i, h)), kv(), kv()],
        out_specs=pl.BlockSpec((tq, hg * SB_HEAD_DIM), lambda b, h, i: (b * nq + i, h)),
        out_shape=jax.ShapeDtypeStruct((B * L, SB_HEADS * SB_HEAD_DIM), BF16),
        scratch_shapes=[pltpu.VMEM((hg, tq, SB_HEAD_DIM), F32), pltpu.VMEM((hg, tq, 1), F32),
                        pltpu.VMEM((hg, tq, tq), F32), pltpu.VMEM((hg, tq, tq), F32),
                        pltpu.VMEM((hg, tq, tq), BF16), pltpu.VMEM((hg, tq, tq), BF16),
                        pltpu.VMEM((hg, tq, tq), BF16)],
        compiler_params=_params("parallel", "parallel", "arbitrary"),
        name="sb_attention_prompt",
    )(q, k, v)


def _sb_decode_kernel(qbd_ref, kn_ref, vn_ref, pk_ref, pv_ref, o_ref, knew_ref, vnew_ref, acc_ref,
                      *, ls, npast):
    H, D = SB_HEADS, SB_HEAD_DIM
    W = H * ls
    qbd = qbd_ref[...]
    scale = D ** -0.5
    acc_ref[...] = jnp.zeros_like(acc_ref)

    def block(k_row, v_row, nb, masked, carry):
        zt = _dot(k_row, qbd) * scale
        sp = _softplus_fast(zt)
        if masked:
            j = lax.broadcasted_iota(jnp.int32, (nb, W), 0)
            t = jnp.bitwise_and(lax.broadcasted_iota(jnp.int32, (nb, W), 1), ls - 1)
            valid = j < t
            sp = jnp.where(valid, sp, 0.0)
        rs = lax.broadcasted_iota(jnp.int32, (nb, nb), 0)
        rj = lax.broadcasted_iota(jnp.int32, (nb, nb), 1)
        tri_t = jnp.where(rj >= rs, 1.0, 0.0).astype(BF16)
        hi, lo = _split2(sp)
        cs = _dot(tri_t, hi) + _dot(tri_t, lo)
        w = jnp.exp(zt - cs - carry)
        if masked:
            w = jnp.where(valid, w, 0.0)
        acc_ref[...] += _dot(w.T.astype(BF16), v_row)
        return carry + cs[0:1, :]

    def rows(ref, k0, n):
        return jnp.concatenate(
            [ref[pl.ds(k0 * H + h, n, stride=H), :] for h in range(H)], axis=1).astype(BF16)

    knew_ref[...] = jnp.zeros_like(knew_ref)
    vnew_ref[...] = jnp.zeros_like(vnew_ref)
    knew_ref[0:ls, :] = rows(kn_ref, 0, ls)
    vnew_ref[0:ls, :] = rows(vn_ref, 0, ls)
    carry = block(knew_ref[...], vnew_ref[...], LANES, True, jnp.zeros((1, W), F32))

    def body(it, carry):
        k0 = pl.multiple_of((npast - 1 - it) * SB_SUB, SB_SUB)
        return block(rows(pk_ref, k0, SB_SUB), rows(pv_ref, k0, SB_SUB), SB_SUB, False, carry)

    lax.fori_loop(0, npast, body, carry)
    for h in range(H):
        o_ref[:, h * D:(h + 1) * D] = acc_ref[h * ls:(h + 1) * ls, h * D:(h + 1) * D].astype(o_ref.dtype)


def sb_attention_decode(q, kn, vn, cache_k, cache_v, layer, B, ls):
    H, D = SB_HEADS, SB_HEAD_DIM
    depth, _, P = cache_k.shape[:3]
    W = H * ls
    assert P % SB_SUB == 0 and ls <= LANES and ls & (ls - 1) == 0 and ls % SUBLANES == 0
    qbd = jnp.einsum("bthd,hg->bhdgt", q.reshape(B, ls, H, D), jnp.eye(H, dtype=q.dtype))
    qbd = qbd.reshape(B, H * D, W)
    row = pl.BlockSpec((ls, H * D), lambda b: (b, 0))
    new = pl.BlockSpec((ls * H, D), lambda b: (b, 0))
    past = pl.BlockSpec((None, None, P * H, D), lambda b: (layer, b, 0, 0))
    return pl.pallas_call(
        functools.partial(_sb_decode_kernel, ls=ls, npast=P // SB_SUB),
        grid=(B,),
        in_specs=[pl.BlockSpec((None, H * D, W), lambda b: (b, 0, 0)), new, new, past, past],
        out_specs=row,
        out_shape=jax.ShapeDtypeStruct((B * ls, H * D), BF16),
        scratch_shapes=[pltpu.VMEM((LANES, H * D), BF16), pltpu.VMEM((LANES, H * D), BF16),
                        pltpu.VMEM((W, H * D), F32)],
        compiler_params=_params("parallel"),
        name="sb_attention_decode",
    )(qbd, kn, vn, cache_k.reshape(depth, B, P * H, D), cache_v.reshape(depth, B, P * H, D))


def _conv_taps(xp_ref, x3, halo, K, sl):
    xp_ref[:, SUBLANES - (K - 1):SUBLANES, :] = halo
    xp_ref[:, SUBLANES:SUBLANES + sl, :] = x3
    base = SUBLANES - (K - 1)
    return [xp_ref[:, base + k:base + k + sl, :] for k in range(K)]


def conv_halos(x, state, B, L, tile):
    C = x.shape[1]
    km1 = state.shape[1]
    nt = L // tile
    if nt == 1:
        return state[:, None]
    tails = x.reshape(B, nt, tile, C)[:, :-1, tile - km1:, :]
    return jnp.concatenate([state[:, None], tails], axis=1)


def _lru_kernel(lx_ref, lg_ref, halo_ref, h0_ref, cw_ref, cb_ref, wa_ref, ba_ref, wx_ref, bx_ref,
                lam_ref, o_ref, hl_ref, xp_ref, a_ref, u_ref, h_ref, hc_ref, *, tl, nl):
    i = pl.program_id(1)
    W = lx_ref.shape[1]

    @pl.when(i == 0)
    def _():
        hc_ref[...] = h0_ref[...]

    taps = _conv_taps(xp_ref, lx_ref[...][None], halo_ref[...][None], 4, tl)
    xc = cb_ref[...]
    for k in range(4):
        xc = xc + cw_ref[k:k + 1, :] * taps[k][0]

    bd = W // LRU_BLOCKS
    for n in range(LRU_BLOCKS):
        sl = slice(n * bd, (n + 1) * bd)
        xn = xc[:, sl]
        xb = xn.astype(BF16)
        r = jax.nn.sigmoid(_dot(xb, wa_ref[n].astype(BF16)) + ba_ref[:, sl])
        ig = jax.nn.sigmoid(_dot(xb, wx_ref[n].astype(BF16)) + bx_ref[:, sl])
        log_a = LRU_C * r * (-_softplus(-lam_ref[:, sl]))
        a = jnp.exp(log_a)
        u = jnp.sqrt(-jnp.tanh(log_a) * (a * a + 1.0)) * (ig * xn)
        a_ref[:, sl] = a
        u_ref[:, sl] = u

    def step(t, h):
        h = a_ref[pl.ds(t, 1), :] * h + u_ref[pl.ds(t, 1), :]
        h_ref[pl.ds(t, 1), :] = h
        return h

    h = lax.fori_loop(0, tl, step, hc_ref[...], unroll=8)
    hc_ref[...] = h
    o_ref[...] = (h_ref[...] * _gelu_tanh(lg_ref[...])).astype(o_ref.dtype)

    @pl.when(i == nl - 1)
    def _():
        hl_ref[...] = h


def lru_branch(lx, lg, state, h0, cw, cb, wa, ba, wx, bx, lam, B, L, tl=256):
    W = lx.shape[1]
    tl = min(tl, L)
    nl = L // tl
    halo = conv_halos(lx, state, B, L, tl)
    kern = functools.partial(_lru_kernel, tl=tl, nl=nl)
    vec = lambda: pl.BlockSpec((1, W), lambda b, i: (0, 0))
    blk = lambda: pl.BlockSpec(wa.shape, lambda b, i: (0, 0, 0))
    o, hl = pl.pallas_call(
        kern,
        grid=(B, nl),
        in_specs=[
            pl.BlockSpec((tl, W), lambda b, i: (b * nl + i, 0)),
            pl.BlockSpec((tl, W), lambda b, i: (b * nl + i, 0)),
            pl.BlockSpec((None, None, 3, W), lambda b, i: (b, i, 0, 0)),
            pl.BlockSpec((None, 1, W), lambda b, i: (b, 0, 0)),
            pl.BlockSpec((4, W), lambda b, i: (0, 0)),
            vec(), blk(), vec(), blk(), vec(), vec(),
        ],
        out_specs=[
            pl.BlockSpec((tl, W), lambda b, i: (b * nl + i, 0)),
            pl.BlockSpec((None, 1, W), lambda b, i: (b, 0, 0)),
        ],
        out_shape=[
            jax.ShapeDtypeStruct((B * L, W), BF16),
            jax.ShapeDtypeStruct((B, 1, W), F32),
        ],
        scratch_shapes=[
            pltpu.VMEM((1, tl + SUBLANES, W), F32),
            pltpu.VMEM((tl, W), F32),
            pltpu.VMEM((tl, W), F32),
            pltpu.VMEM((tl, W), F32),
            pltpu.VMEM((1, W), F32),
        ],
        compiler_params=_params("parallel", "arbitrary"),
        name="lru_branch",
    )(lx, lg, halo, h0.reshape(B, 1, W), cw, cb.reshape(1, W), wa, ba.reshape(1, W),
      wx, bx.reshape(1, W), lam.reshape(1, W))
    return o, hl.reshape(B, W)


def _ssd_kernel(z_ref, xbc_ref, dt_ref, halo_ref, h0_ref, cw_ref, cb_ref, dtb_ref, alog_ref,
                dexp_ref, nw_ref, o_ref, hl_ref, xp_ref, y_ref, st_ref, *, q, nc, n_valid):
    c = pl.program_id(1)
    inner = SSD_HEADS * SSD_HEAD_DIM
    gw = inner // SSD_GROUPS
    epg = SSD_HEADS // SSD_GROUPS
    n = SSD_STATE

    @pl.when(c == 0)
    def _():
        st_ref[...] = h0_ref[...]

    taps = _conv_taps(xp_ref, xbc_ref[...][None], halo_ref[...][None], 4, q)
    xc = cb_ref[...]
    for k in range(4):
        xc = xc + cw_ref[k:k + 1, :] * taps[k][0]
    xc = _silu(xc)
    xs = xc[:, :inner]
    bm = xc[:, inner:inner + SSD_GROUPS * n]
    cm = xc[:, inner + SSD_GROUPS * n:]

    rowi = lax.broadcasted_iota(jnp.int32, (q, LANES), 0)
    lanei = lax.broadcasted_iota(jnp.int32, (q, LANES), 1)
    dts = jnp.where(lanei < SSD_HEADS, _softplus(dt_ref[...] + dtb_ref[...]), 0.0)
    if n_valid < q:
        dts = jnp.where(rowi < n_valid, dts, 0.0)
    dta = dts * (-jnp.exp(alog_ref[...]))

    ti = lax.broadcasted_iota(jnp.int32, (q, q), 0)
    si = lax.broadcasted_iota(jnp.int32, (q, q), 1)
    causal = si <= ti
    lower = jnp.where(causal, 1.0, 0.0).astype(BF16)
    cum = sum(_dot(lower, p) for p in _split3(dta))
    cum_t = cum.T
    cum_last = cum[q - 1:q, :]
    exp_cum = jnp.exp(cum)
    decay_end = jnp.exp(cum_last - cum)

    eh = lax.broadcasted_iota(jnp.int32, (LANES, inner), 0)
    ec = lax.broadcasted_iota(jnp.int32, (LANES, inner), 1)
    expand = jnp.where(ec // SSD_HEAD_DIM == eh, 1.0, 0.0).astype(BF16)

    def expand_heads(a):
        return sum(_dot(p, expand) for p in _split3(a))

    dts_e = expand_heads(dts)
    exp_cum_e = expand_heads(exp_cum)
    decay_end_e = expand_heads(decay_end)
    chunk_decay_e = exp_cum_e[q - 1:q, :]

    xdt = xs * dts_e
    xdt_b = xdt.astype(BF16)
    xend_b = (decay_end_e * xdt).astype(BF16)

    for g in range(SSD_GROUPS):
        bg = bm[:, g * n:(g + 1) * n]
        cg_b = cm[:, g * n:(g + 1) * n].astype(BF16)
        gs = slice(g * gw, (g + 1) * gw)
        cb = _dot_nt(cg_b, bg.astype(BF16))
        st = st_ref[g]
        y_inter = _dot(cg_b, st.astype(BF16)) * exp_cum_e[:, gs]
        for e in range(epg):
            h = g * epg + e
            hs = slice(h * SSD_HEAD_DIM, (h + 1) * SSD_HEAD_DIM)
            seg = cum[:, h:h + 1] - cum_t[h:h + 1, :]
            decay = jnp.where(causal, jnp.exp(seg), 0.0)
            wts = (cb * decay).astype(BF16)
            y_ref[:, hs] = _dot(wts, xdt_b[:, hs]) + y_inter[:, e * SSD_HEAD_DIM:(e + 1) * SSD_HEAD_DIM]
        bg_t = bg.T.astype(BF16)
        st_ref[g] = chunk_decay_e[:, gs] * st + _dot(bg_t, xend_b[:, gs])

    y = y_ref[...] + dexp_ref[...] * xs
    yg = y * _silu(z_ref[...])
    for g in range(SSD_GROUPS):
        gs = slice(g * gw, (g + 1) * gw)
        part = yg[:, gs]
        ms = jnp.mean(part * part, axis=-1, keepdims=True)
        o_ref[:, gs] = (part * lax.rsqrt(ms + NORM_EPS) * nw_ref[:, gs]).astype(o_ref.dtype)

    @pl.when(c == nc - 1)
    def _():
        hl_ref[...] = st_ref[...]


def _pad_rows(x, B, L, Lp):
    if Lp == L:
        return x
    C = x.shape[1]
    return jnp.pad(x.reshape(B, L, C), ((0, 0), (0, Lp - L), (0, 0))).reshape(B * Lp, C)


def ssd_branch(z, xbc, dt, dt_block, state, h0, cw, cb, dt_bias, a_log, d, norm_w, B, L, q=128):
    inner = z.shape[1]
    cd = xbc.shape[1]
    nh = SSD_HEADS
    Lp = max(L, q) if L % q else L
    n_valid = min(L, q)
    nc = Lp // q
    gw = inner // SSD_GROUPS
    epg = SSD_HEADS // SSD_GROUPS
    halo = conv_halos(xbc, state, B, L, q) if L >= q else state[:, None]
    zp = _pad_rows(z, B, L, Lp)
    xbcp = _pad_rows(xbc, B, L, Lp)
    if Lp != L:
        dt, dt_block = dt[:, dt_block * LANES:(dt_block + 1) * LANES], 0
    dtp = _pad_rows(dt, B, L, Lp)
    h0t = h0.reshape(B, SSD_GROUPS, epg, SSD_HEAD_DIM, SSD_STATE).transpose(0, 1, 4, 2, 3)
    h0t = h0t.reshape(B, SSD_GROUPS, SSD_STATE, gw)
    pad1 = lambda a: jnp.pad(a, (0, LANES - nh)).reshape(1, LANES)
    kern = functools.partial(_ssd_kernel, q=q, nc=nc, n_valid=n_valid)
    row = lambda w: pl.BlockSpec((q, w), lambda b, c: (b * nc + c, 0))
    vec = lambda w: pl.BlockSpec((1, w), lambda b, c: (0, 0))
    o, hl = pl.pallas_call(
        kern,
        grid=(B, nc),
        in_specs=[
            row(inner), row(cd), pl.BlockSpec((q, LANES), lambda b, c: (b * nc + c, dt_block)),
            pl.BlockSpec((None, None, 3, cd), lambda b, c: (b, c, 0, 0)),
            pl.BlockSpec((None, SSD_GROUPS, SSD_STATE, gw), lambda b, c: (b, 0, 0, 0)),
            pl.BlockSpec((4, cd), lambda b, c: (0, 0)),
            vec(cd), vec(LANES), vec(LANES), vec(inner), vec(inner),
        ],
        out_specs=[
            row(inner),
            pl.BlockSpec((None, SSD_GROUPS, SSD_STATE, gw), lambda b, c: (b, 0, 0, 0)),
        ],
        out_shape=[
            jax.ShapeDtypeStruct((B * Lp, inner), BF16),
            jax.ShapeDtypeStruct((B, SSD_GROUPS, SSD_STATE, gw), F32),
        ],
        scratch_shapes=[
            pltpu.VMEM((1, q + SUBLANES, cd), F32),
            pltpu.VMEM((q, inner), F32),
            pltpu.VMEM((SSD_GROUPS, SSD_STATE, gw), F32),
        ],
        compiler_params=_params("parallel", "arbitrary"),
        name="ssd_branch",
    )(zp, xbcp, dtp, halo, h0t, cw, cb.reshape(1, cd), pad1(dt_bias), pad1(a_log),
      jnp.repeat(d, SSD_HEAD_DIM).reshape(1, inner), norm_w.reshape(1, inner))
    if Lp != L:
        o = o.reshape(B, Lp, inner)[:, :L].reshape(B * L, inner)
    hl = hl.reshape(B, SSD_GROUPS, SSD_STATE, epg, SSD_HEAD_DIM).transpose(0, 1, 3, 4, 2)
    return o, hl.reshape(B, SSD_HEADS, SSD_HEAD_DIM, SSD_STATE)


def _merge_kernel(x_ref, g_ref, a_ref, b_ref, c_ref, wg0_ref, wg1_ref, wg2_ref, wb_ref, o_ref, xn_ref):
    @pl.when(pl.program_id(1) == 0)
    def _():
        x = x_ref[...]
        ms = jnp.mean(x * x, axis=-1, keepdims=True)
        xn_ref[...] = (x * lax.rsqrt(ms + NORM_EPS) * g_ref[...]).astype(BF16)

    xn = xn_ref[...]
    acc = None
    for n, (br, wg) in enumerate(((a_ref, wg0_ref), (b_ref, wg1_ref), (c_ref, wg2_ref))):
        gate = _dot_nt(xn, wg[...].astype(BF16))
        t = _dot(br[...], wb_ref[n].astype(BF16)) * jax.nn.sigmoid(gate)
        acc = t if acc is None else acc + t
    o_ref[...] = acc.astype(o_ref.dtype)


def gated_merge(x, g, oa, ob, oc, w_gate_t, wb, layer, tm=1024, tn=256):
    T, K = oa.shape
    D = x.shape[1]
    N = wb.shape[-1]
    tm = min(tm, T)
    nj = N // tn
    br = lambda: pl.BlockSpec((tm, K), lambda i, j: (i, 0))
    gate = lambda n: pl.BlockSpec((None, tn, D), lambda i, j: (layer, n * nj + j, 0))
    return pl.pallas_call(
        _merge_kernel,
        grid=(T // tm, nj),
        in_specs=[pl.BlockSpec((tm, D), lambda i, j: (i, 0), pipeline_mode=pl.Buffered(1)),
                  pl.BlockSpec((1, D), lambda i, j: (0, 0)),
                  br(), br(), br(), gate(0), gate(1), gate(2),
                  pl.BlockSpec((None, 3, K, tn), lambda i, j: (layer, 0, 0, j))],
        out_specs=pl.BlockSpec((tm, tn), lambda i, j: (i, j)),
        out_shape=jax.ShapeDtypeStruct((T, N), BF16),
        scratch_shapes=[pltpu.VMEM((tm, D), BF16)],
        compiler_params=_params("parallel", "arbitrary"),
        name="gated_merge",
    )(x, g.reshape(1, D), oa, ob, oc, w_gate_t, w_gate_t, w_gate_t, wb)


def _xattn_kernel(q_ref, mk_ref, mv_ref, o_ref):
    hd = q_ref.shape[1] // XA_HEADS
    scale = hd ** -0.5
    for h in range(XA_HEADS):
        sl = slice(h * hd, (h + 1) * hd)
        kh, vh = (mk_ref[:, sl], mv_ref[:, sl]) if len(mk_ref.shape) == 2 else (mk_ref[:, h, :], mv_ref[:, h, :])
        s = _dot_nt(q_ref[:, sl].astype(BF16), kh.astype(BF16)) * scale
        m = jnp.max(s, axis=-1, keepdims=True)
        p = jnp.exp(s - m)
        p = p / jnp.sum(p, axis=-1, keepdims=True)
        o_ref[:, sl] = _dot(p.astype(BF16), vh.astype(BF16)).astype(o_ref.dtype)


def cross_attention(q, mk, mv, layer, B, L, tq=512):
    D = q.shape[1]
    tq = min(tq, L)
    nq = L // tq
    if layer is None:
        mem = lambda: pl.BlockSpec((None,) + mk.shape[1:], lambda b, i: (b, 0, 0))
    else:
        mem = lambda: pl.BlockSpec((None, None) + mk.shape[2:], lambda b, i: (layer, b, 0, 0, 0))
    return pl.pallas_call(
        _xattn_kernel,
        grid=(B, nq),
        in_specs=[pl.BlockSpec((tq, D), lambda b, i: (b * nq + i, 0)), mem(), mem()],
        out_specs=pl.BlockSpec((tq, D), lambda b, i: (b * nq + i, 0)),
        out_shape=jax.ShapeDtypeStruct((B * L, D), BF16),
        compiler_params=_params("parallel", "arbitrary"),
        name="cross_attention",
    )(q, mk, mv)


def _mem_kv_kernel(x_ref, g_ref, wk_ref, wv_ref, k_ref, v_ref, k2_ref, v2_ref, xn_ref):
    j = pl.program_id(1)

    @pl.when(j == 0)
    def _():
        x = x_ref[...]
        ms = jnp.mean(x * x, axis=-1, keepdims=True)
        xn_ref[...] = (x * lax.rsqrt(ms + NORM_EPS) * g_ref[...]).astype(BF16)

    xn = xn_ref[...]
    k = _dot(xn, wk_ref[...].astype(BF16))
    v = _dot(xn, wv_ref[...].astype(BF16))
    k2_ref[...] = k.astype(k2_ref.dtype)
    v2_ref[...] = v.astype(v2_ref.dtype)
    for h in range(XA_HEADS):
        @pl.when(j == h)
        def _(h=h):
            k_ref[:, h, :] = k
            v_ref[:, h, :] = v


def mem_kv(mem, g, wk, wv, layer, B, M):
    D = mem.shape[1]
    hd = D // XA_HEADS
    w = lambda: pl.BlockSpec((None, D, hd), lambda b, j: (layer, 0, j))
    o4 = lambda: pl.BlockSpec((None, M, XA_HEADS, hd), lambda b, j: (b, 0, 0, 0))
    o2 = lambda: pl.BlockSpec((None, M, hd), lambda b, j: (b, 0, j))
    return pl.pallas_call(
        _mem_kv_kernel,
        grid=(B, XA_HEADS),
        in_specs=[pl.BlockSpec((M, D), lambda b, j: (b, 0)), pl.BlockSpec((1, D), lambda b, j: (0, 0)),
                  w(), w()],
        out_specs=[o4(), o4(), o2(), o2()],
        out_shape=[jax.ShapeDtypeStruct((B, M, XA_HEADS, hd), F32)] * 2
        + [jax.ShapeDtypeStruct((B, M, D), BF16)] * 2,
        scratch_shapes=[pltpu.VMEM((M, D), BF16)],
        compiler_params=_params("parallel", "arbitrary"),
        name="mem_kv",
    )(mem, g.reshape(1, D), wk, wv)


def _ffn_up_kernel(x_ref, g_ref, wg_ref, wv_ref, sg_ref, sv_ref, cwg_ref, cwv_ref, cbg_ref, cbv_ref,
                   act_ref, tg_ref, tv_ref, xn_ref, xa_ref, xb_ref, carry_ref, *, nseg, sl, tpb, nf, npos):
    s = pl.program_id(0)
    tm, tf = act_ref.shape
    ts = xa_ref.shape[-1]
    base = SUBLANES - 2
    rc = min(sl, CHUNK_ROWS)
    fc = jnp.minimum(s, npos - 1) % nf
    prev = jnp.maximum(s - 1, 0)
    ip, fp = prev // nf, prev % nf

    @pl.when(s == 0)
    def _():
        xa_ref[...] = jnp.zeros_like(xa_ref)
        xb_ref[...] = jnp.zeros_like(xb_ref)
        carry_ref[...] = jnp.zeros_like(carry_ref)

    @pl.when((fc == 0) & (s < npos))
    def _():
        x = x_ref[...]
        ms = jnp.mean(x * x, axis=-1, keepdims=True)
        xn_ref[...] = (x * lax.rsqrt(ms + NORM_EPS) * g_ref[...]).astype(BF16)

    def step(xw_ref, xr_ref):
        xn = xn_ref[...]
        starts = ip % tpb == 0

        rb = min(tm, MXU_WIDTH)
        assert sl % rb == 0 or (tm == rb and rb % sl == 0)

        def matmul(sub, c, w, r0):
            u = _dot(xn_ref[r0:r0 + rb, :], w)
            if sl >= rb:
                seg, s0 = r0 // sl, r0 % sl
                xw_ref[2 * sub + c, seg, SUBLANES + s0:SUBLANES + s0 + rb, :] = u
            else:
                xw_ref[2 * sub + c, :, SUBLANES:SUBLANES + sl, :] = u.reshape(nseg, sl, ts)

        def edges(sub, c, st_ref, tail_ref):
            cols = slice(sub * ts, (sub + 1) * ts)
            xp = xr_ref.at[2 * sub + c]
            halo = st_ref[:, :, cols]
            if tpb > 1:
                halo = jnp.where(starts, halo, carry_ref[c, fp, :, :, cols])
            xp[:, base:SUBLANES, :] = halo
            tail = xp[:, SUBLANES + sl - 2:SUBLANES + sl, :]
            if tpb > 1:
                carry_ref[c, fp, :, :, cols] = tail
            tail_ref[:, :, cols] = tail

        def conv_swiglu(sub, chunks):
            cols = slice(sub * ts, (sub + 1) * ts)
            xg, xv = xr_ref.at[2 * sub], xr_ref.at[2 * sub + 1]
            for seg, r0 in chunks:
                gate, val = cbg_ref[:, cols], cbv_ref[:, cols]
                for k in range(3):
                    rows = slice(base + k + r0, base + k + r0 + rc)
                    gate = gate + cwg_ref[k:k + 1, cols] * xg[seg, rows, :]
                    val = val + cwv_ref[k:k + 1, cols] * xv[seg, rows, :]
                act_ref[seg * sl + r0:seg * sl + r0 + rc, cols] = (_silu(gate) * val).astype(act_ref.dtype)

        chunks = [(seg, r0) for seg in range(nseg) for r0 in range(0, sl, rc)]
        npieces = 2 * (tm // rb)
        per = -(-len(chunks) // npieces)
        for sub in range(tf // ts):
            cols = slice(sub * ts, (sub + 1) * ts)
            edges(sub, 0, sg_ref, tg_ref)
            edges(sub, 1, sv_ref, tv_ref)
            piece = 0
            for c, w_ref in ((0, wg_ref), (1, wv_ref)):
                w = w_ref[:, cols].astype(BF16)
                for r0 in range(0, tm, rb):
                    matmul(sub, c, w, r0)
                    conv_swiglu(sub, chunks[piece * per:(piece + 1) * per])
                    piece += 1

    @pl.when(s % 2 == 0)
    def _():
        step(xa_ref, xb_ref)

    @pl.when(s % 2 == 1)
    def _():
        step(xb_ref, xa_ref)


def ffn_up(x, g, w_up, layer, state, cw, cb, B, L, tm=1024, tf=512):
    T, D = x.shape
    F2 = w_up.shape[-1]
    F = F2 // 2
    if L >= tm:
        nseg, sl, tpb = 1, tm, L // tm
    else:
        nseg, sl, tpb = min(tm // L, B), L, 1
        tm = nseg * sl
    nt, nf = T // tm, F // tf
    ts = min(tf, MXU_WIDTH)
    st = state.reshape(B // nseg, nseg, 2, F2)
    cb2 = cb.reshape(1, F2)
    npos = nt * nf
    kern = functools.partial(_ffn_up_kernel, nseg=nseg, sl=sl, tpb=tpb, nf=nf, npos=npos)
    cur = lambda s: jnp.minimum(s, npos - 1)
    prv = lambda s: jnp.maximum(s - 1, 0)
    wspec = lambda off: pl.BlockSpec((None, D, tf), lambda s: (layer, 0, off + cur(s) % nf))
    sspec = lambda off: pl.BlockSpec((None, nseg, 2, tf),
                                     lambda s: (prv(s) // nf // tpb, 0, 0, off + prv(s) % nf))
    cwspec = lambda off: pl.BlockSpec((3, tf), lambda s: (0, off + prv(s) % nf))
    cbspec = lambda off: pl.BlockSpec((1, tf), lambda s: (0, off + prv(s) % nf))
    tspec = lambda: pl.BlockSpec((None, nseg, 2, tf), lambda s: (prv(s) // nf, 0, 0, prv(s) % nf))
    xscratch = lambda: pltpu.VMEM((2 * (tf // ts), nseg, sl + SUBLANES, ts), F32)
    act, tg, tv = pl.pallas_call(
        kern,
        grid=(npos + 1,),
        in_specs=[
            pl.BlockSpec((tm, D), lambda s: (cur(s) // nf, 0), pipeline_mode=pl.Buffered(1)),
            pl.BlockSpec((1, D), lambda s: (0, 0)),
            wspec(0), wspec(nf), sspec(0), sspec(nf), cwspec(0), cwspec(nf), cbspec(0), cbspec(nf),
        ],
        out_specs=[pl.BlockSpec((tm, tf), lambda s: (prv(s) // nf, prv(s) % nf)), tspec(), tspec()],
        out_shape=[
            jax.ShapeDtypeStruct((T, F), BF16),
            jax.ShapeDtypeStruct((nt, nseg, 2, F), F32),
            jax.ShapeDtypeStruct((nt, nseg, 2, F), F32),
        ],
        scratch_shapes=[
            pltpu.VMEM((tm, D), BF16),
            xscratch(), xscratch(),
            pltpu.VMEM((2, nf, nseg, 2, tf), F32),
        ],
        compiler_params=_params("arbitrary"),
        name="ffn_up",
    )(x, g.reshape(1, D), w_up, w_up, st, st, cw, cw, cb2, cb2)
    last = lambda t: t.reshape(B // nseg, tpb, nseg, 2, F)[:, -1].reshape(B, 2, F)
    return act, jnp.concatenate([last(tg), last(tv)], axis=-1)


def _last_rows(x, state, B, L):
    km1 = state.shape[1]
    x3 = x.reshape(B, L, x.shape[1])
    if L >= km1:
        return x3[:, L - km1:]
    return jnp.concatenate([state, x3], axis=1)[:, -km1:]


def trunk_layer(x, B, L, mk, mv, mem_layer, past_k, past_v, lru_buf, lru_h, ssd_buf, ssd_h, ffn_buf, w, l):
    mix = SB_HEADS * SB_HEAD_DIM
    inner = SSD_HEADS * SSD_HEAD_DIM
    cd = inner + 2 * SSD_GROUPS * SSD_STATE
    D = x.shape[1]
    tn = 512
    q, k, v, lx, lg, z, xbc, dtx = in_proj(x, w["norm_mix"][l], w["w_in_t"], l,
                                           (mix,) * 5 + (inner, cd, tn), (BF16,) + (F32,) * 7,
                                           (False, True, True) + (False,) * 5, SB_HEAD_DIM, tn=tn)

    if past_k is None:
        o_a = sb_attention_prompt(q, k, v, B, L)
    else:
        o_a = sb_attention_decode(q, k, v, past_k, past_v, l, B, L)
    o_b, lru_h_new = lru_branch(lx, lg, lru_buf, lru_h, w["lru_conv_w"][l], w["lru_conv_b"][l], w["lru_wa"][l],
                                w["lru_ba"][l], w["lru_wx"][l], w["lru_bx"][l], w["lru_lam"][l], B, L)
    o_c, ssd_h_new = ssd_branch(z, xbc, dtx, 0, ssd_buf, ssd_h, w["ssd_conv_w"][l],
                                w["ssd_conv_b"][l], w["ssd_dt_bias"][l], w["ssd_a_log"][l], w["ssd_d"][l],
                                w["ssd_norm"][l], B, L)
    merged = gated_merge(x, w["norm_mix"][l], o_a, o_b, o_c, w["w_gate_t"], w["w_branch"], l)
    h = matmul_res(merged, w["w_out"], l, x, tm=2048)
    qx = norm_matmul(h, w["norm_xattn"][l], w["w_xq"], l, out_dtype=BF16)
    o_x = cross_attention(qx, mk, mv, mem_layer, B, L)
    h = matmul_res(o_x, w["w_xo"], l, h, tm=2048)
    act, ffn_new = ffn_up(h, w["norm_ffn"][l], w["w_up"], l, ffn_buf, w["ffn_conv_w"][l],
                          w["ffn_conv_b"][l], B, L)
    out = matmul_res(act, w["w_down"], l, h, tn=256)
    states = (k.reshape(B, L, SB_HEADS, SB_HEAD_DIM), v.reshape(B, L, SB_HEADS, SB_HEAD_DIM),
              _last_rows(lx, lru_buf, B, L), lru_h_new, _last_rows(xbc, ssd_buf, B, L), ssd_h_new,
              ffn_new)
    return out, states


def kernel(x_prompt, x_sample, mem_prompt, cache_sb_k, cache_sb_v, cache_mem_k, cache_mem_v, state_lru_conv, state_lru_h, state_ssd_conv, state_ssd, state_ffn_conv, norm_mix, w_in, lru_conv_w, lru_conv_b, lru_wa, lru_ba, lru_wx, lru_bx, lru_lam, ssd_conv_w, ssd_conv_b, ssd_dt_bias, ssd_a_log, ssd_d, ssd_norm, w_branch, w_out, norm_xattn, norm_mem, w_xq, w_xk, w_xv, w_xo, norm_ffn, w_up, ffn_conv_w, ffn_conv_b, w_down, norm_final):
    depth = w_in.shape[0]
    Bp, Lp, D = x_prompt.shape
    Bs, Ls, _ = x_sample.shape
    M = mem_prompt.shape[1]
    mix = SB_HEADS * SB_HEAD_DIM
    inner = SSD_HEADS * SSD_HEAD_DIM
    cd = inner + 2 * SSD_GROUPS * SSD_STATE
    f2 = w_up.shape[2]
    xa_hd = D // XA_HEADS

    yp = x_prompt.reshape(Bp * Lp, D)
    ys = x_sample.reshape(Bs * Ls, D)
    mem = mem_prompt.reshape(Bp * M, D)
    st_p, st_s, mk_list, mv_list = [], [], [], []
    w_in_t = jnp.swapaxes(w_in, 1, 2)
    w_gate_t = w_in_t[:, 5 * mix + inner + cd + SSD_HEADS:]
    w = dict(w_in_t=w_in_t, w_gate_t=w_gate_t, norm_mix=norm_mix, lru_conv_w=lru_conv_w, lru_conv_b=lru_conv_b,
             lru_wa=lru_wa, lru_ba=lru_ba, lru_wx=lru_wx, lru_bx=lru_bx, lru_lam=lru_lam,
             ssd_conv_w=ssd_conv_w, ssd_conv_b=ssd_conv_b, ssd_dt_bias=ssd_dt_bias, ssd_a_log=ssd_a_log,
             ssd_d=ssd_d, ssd_norm=ssd_norm, w_branch=w_branch, w_out=w_out, norm_xattn=norm_xattn,
             w_xq=w_xq, w_xo=w_xo, norm_ffn=norm_ffn, w_up=w_up, ffn_conv_w=ffn_conv_w,
             ffn_conv_b=ffn_conv_b, w_down=w_down)
    for l in range(depth):
        mk_p, mv_p, mk2, mv2 = mem_kv(mem, norm_mem[l], w_xk, w_xv, l, Bp, M)
        zeros = lambda *s: jnp.zeros(s, F32)
        yp, sp = trunk_layer(yp, Bp, Lp, mk2, mv2, None, None, None, zeros(Bp, 3, mix),
                             zeros(Bp, mix), zeros(Bp, 3, cd), zeros(Bp, SSD_HEADS, SSD_HEAD_DIM, SSD_STATE),
                             zeros(Bp, 2, f2), w, l)
        ys, ss = trunk_layer(ys, Bs, Ls, cache_mem_k, cache_mem_v, l,
                             cache_sb_k, cache_sb_v, state_lru_conv[l], state_lru_h[l],
                             state_ssd_conv[l], state_ssd[l], state_ffn_conv[l], w, l)
        st_p.append(sp)
        st_s.append(ss)
        mk_list.append(mk_p)
        mv_list.append(mv_p)
    y_prompt = rmsnorm(yp, norm_final).reshape(Bp, Lp, D)
    y_sample = rmsnorm(ys, norm_final).reshape(Bs, Ls, D)

    def stk(lst, i):
        return jnp.stack([s[i] for s in lst], axis=0)

    return (y_prompt, y_sample,
            stk(st_p, 0), stk(st_p, 1), stk(st_p, 2), stk(st_p, 3), stk(st_p, 4), stk(st_p, 5), stk(st_p, 6),
            jnp.stack(mk_list, axis=0), jnp.stack(mv_list, axis=0),
            stk(st_s, 0), stk(st_s, 1), stk(st_s, 2), stk(st_s, 3), stk(st_s, 4), stk(st_s, 5), stk(st_s, 6))
```

```python
import functools
import math

import jax
import jax.numpy as jnp
from jax import lax
from jax.experimental import pallas as pl
from jax.experimental.pallas import tpu as pltpu

F32 = jnp.float32
BF16 = jnp.bfloat16

NORM_EPS = 1e-6
LRU_C = 8.0
LANES = 128
SUBLANES = 8
MXU_WIDTH = 256
CHUNK_ROWS = 32
VMEM_LIMIT_BYTES = 56 * 1024 * 1024

SB_HEADS = 8
SB_HEAD_DIM = 128
LRU_BLOCKS = 8
SSD_HEADS = 16
SSD_HEAD_DIM = 64
SSD_GROUPS = 2
SSD_STATE = 128
XA_HEADS = 4


def _params(*sem):
    return pltpu.CompilerParams(dimension_semantics=sem, vmem_limit_bytes=VMEM_LIMIT_BYTES)


def _dot(a, b):
    return jnp.dot(a, b, preferred_element_type=F32)


def _dot_nt(a, b):
    return lax.dot_general(a, b, (((1,), (1,)), ((), ())), preferred_element_type=F32)


def _split3(x):
    hi = x.astype(BF16)
    r = x - hi.astype(F32)
    mid = r.astype(BF16)
    lo = (r - mid.astype(F32)).astype(BF16)
    return hi, mid, lo


def _softplus(x):
    return jnp.maximum(x, 0.0) + jnp.log1p(jnp.exp(-jnp.abs(x)))


def _silu(x):
    return x * jax.nn.sigmoid(x)


def _gelu_tanh(x):
    c = math.sqrt(2.0 / math.pi)
    return 0.5 * x * (1.0 + jnp.tanh(c * (x + 0.044715 * (x * x * x))))


def _norm_matmul_kernel(x_ref, g_ref, w_ref, o_ref, xn_ref, *, transposed):
    @pl.when(pl.program_id(1) == 0)
    def _():
        x = x_ref[...]
        ms = jnp.mean(x * x, axis=-1, keepdims=True)
        xn_ref[...] = (x * lax.rsqrt(ms + NORM_EPS) * g_ref[...]).astype(BF16)

    w = w_ref[...].astype(BF16)
    acc = _dot_nt(xn_ref[...], w) if transposed else _dot(xn_ref[...], w)
    o_ref[...] = acc.astype(o_ref.dtype)


def _layer_spec(w, layer, block, index):
    if layer is None:
        return pl.BlockSpec(block, index)
    return pl.BlockSpec((None,) + block, lambda *ids: (layer,) + index(*ids))


def norm_matmul(x, g, w, layer=None, out_dtype=F32, transposed=False, tm=1024, tn=512):
    T, D = x.shape
    N = w.shape[-2] if transposed else w.shape[-1]
    tm = min(tm, T)
    tn = min(tn, N)
    w_spec = (_layer_spec(w, layer, (tn, D), lambda i, j: (j, 0)) if transposed
              else _layer_spec(w, layer, (D, tn), lambda i, j: (0, j)))
    return pl.pallas_call(
        functools.partial(_norm_matmul_kernel, transposed=transposed),
        grid=(T // tm, N // tn),
        in_specs=[
            pl.BlockSpec((tm, D), lambda i, j: (i, 0)),
            pl.BlockSpec((1, D), lambda i, j: (0, 0)),
            w_spec,
        ],
        out_specs=pl.BlockSpec((tm, tn), lambda i, j: (i, j)),
        out_shape=jax.ShapeDtypeStruct((T, N), out_dtype),
        scratch_shapes=[pltpu.VMEM((tm, D), BF16)],
        compiler_params=_params("parallel", "arbitrary"),
        name="norm_matmul",
    )(x, g.reshape(1, D), w)


def _in_proj_kernel(x_ref, g_ref, w_ref, *refs, ranges, by_head, hd):
    out_refs, xn_ref = refs[:-1], refs[-1]
    j = pl.program_id(1)
    tm, tn = xn_ref.shape[0], w_ref.shape[0]

    @pl.when(j == 0)
    def _():
        x = x_ref[...]
        ms = jnp.mean(x * x, axis=-1, keepdims=True)
        xn_ref[...] = (x * lax.rsqrt(ms + NORM_EPS) * g_ref[...]).astype(BF16)

    acc = _dot_nt(xn_ref[...], w_ref[...].astype(BF16))
    for o_ref, (lo, hi), heads in zip(out_refs, ranges, by_head):
        @pl.when((j >= lo) & (j < hi))
        def _(o_ref=o_ref, lo=lo, heads=heads):
            if heads:
                nh = (hi - lo) * (tn // hd)
                for c in range(tn // hd):
                    o_ref[pl.ds((j - lo) * (tn // hd) + c, tm, stride=nh), :] = acc[:, c * hd:(c + 1) * hd]
            else:
                o_ref[...] = acc.astype(o_ref.dtype)


def in_proj(x, g, w_t, layer, widths, dtypes, by_head, hd, tm=1024, tn=512):
    T, D = x.shape
    tm = min(tm, T)
    ranges, nblocks = [], 0
    for wd in widths:
        assert wd % tn == 0
        ranges.append((nblocks, nblocks + wd // tn))
        nblocks += wd // tn
    out_specs, out_shape = [], []
    for (lo, hi), wd, dt, heads in zip(ranges, widths, dtypes, by_head):
        if heads:
            out_specs.append(pl.BlockSpec((tm * (wd // hd), hd), lambda i, j: (i, 0),
                                          pipeline_mode=pl.Buffered(1)))
            out_shape.append(jax.ShapeDtypeStruct((T * (wd // hd), hd), dt))
        else:
            out_specs.append(pl.BlockSpec(
                (tm, tn), lambda i, j, lo=lo, n=hi - lo: (i, jnp.clip(j - lo, 0, n - 1))))
            out_shape.append(jax.ShapeDtypeStruct((T, wd), dt))
    return pl.pallas_call(
        functools.partial(_in_proj_kernel, ranges=tuple(ranges), by_head=tuple(by_head), hd=hd),
        grid=(T // tm, nblocks),
        in_specs=[
            pl.BlockSpec((tm, D), lambda i, j: (i, 0), pipeline_mode=pl.Buffered(1)),
            pl.BlockSpec((1, D), lambda i, j: (0, 0)),
            pl.BlockSpec((None, tn, D), lambda i, j: (layer, j, 0)),
        ],
        out_specs=out_specs,
        out_shape=out_shape,
        scratch_shapes=[pltpu.VMEM((tm, D), BF16)],
        compiler_params=_params("parallel", "arbitrary"),
        name="in_proj",
    )(x, g.reshape(1, D), w_t)


def _matmul_res_kernel(x_ref, w_ref, r_ref, o_ref):
    o_ref[...] = r_ref[...] + _dot(x_ref[...], w_ref[...].astype(BF16))


def matmul_res(x, w, layer, res, tm=1024, tn=512):
    T, K = x.shape
    N = w.shape[-1]
    tm = min(tm, T)
    tn = min(tn, N)
    return pl.pallas_call(
        _matmul_res_kernel,
        grid=(T // tm, N // tn),
        in_specs=[
            pl.BlockSpec((tm, K), lambda i, j: (i, 0)),
            _layer_spec(w, layer, (K, tn), lambda i, j: (0, j)),
            pl.BlockSpec((tm, tn), lambda i, j: (i, j)),
        ],
        out_specs=pl.BlockSpec((tm, tn), lambda i, j: (i, j)),
        out_shape=jax.ShapeDtypeStruct((T, N), F32),
        compiler_params=_params("parallel", "arbitrary"),
        name="matmul_res",
    )(x, w, res)


def _rmsnorm_kernel(x_ref, g_ref, o_ref):
    x = x_ref[...]
    ms = jnp.mean(x * x, axis=-1, keepdims=True)
    o_ref[...] = x * lax.rsqrt(ms + NORM_EPS) * g_ref[...]


def rmsnorm(x, g, tm=512):
    T, D = x.shape
    tm = min(tm, T)
    return pl.pallas_call(
        _rmsnorm_kernel,
        grid=(T // tm,),
        in_specs=[pl.BlockSpec((tm, D), lambda i: (i, 0)), pl.BlockSpec((1, D), lambda i: (0, 0))],
        out_specs=pl.BlockSpec((tm, D), lambda i: (i, 0)),
        out_shape=jax.ShapeDtypeStruct((T, D), F32),
        compiler_params=_params("parallel"),
        name="rmsnorm",
    )(x, g.reshape(1, D))


SB_SUB = MXU_WIDTH


def _softplus_fast(z):
    return jnp.maximum(z, 0.0) + jnp.log(1.0 + jnp.exp(-jnp.abs(z)))


def _split2(x):
    hi = x.astype(BF16)
    return hi, (x - hi.astype(F32)).astype(BF16)


def _sb_prompt_kernel(q_ref, k_ref, v_ref, o_ref, acc_ref, carry_ref, z_ref, cs_ref, hi_ref, lo_ref, w_ref,
                      *, tq, hg):
    h0 = pl.program_id(1) * hg
    qi = pl.program_id(2)
    D = SB_HEAD_DIM
    rj = lax.broadcasted_iota(jnp.int32, (SB_SUB, SB_SUB), 0)
    rs = lax.broadcasted_iota(jnp.int32, (SB_SUB, SB_SUB), 1)
    tri = jnp.where(rj >= rs, 1.0, 0.0).astype(BF16)
    scale = D ** -0.5
    acc_ref[...] = jnp.zeros_like(acc_ref)
    carry_ref[...] = jnp.zeros_like(carry_ref)

    def chunk(c0, masked):
        c0 = pl.multiple_of(c0, tq)
        rc = CHUNK_ROWS

        def valid(r0):
            row = r0 + lax.broadcasted_iota(jnp.int32, (rc, tq), 0)
            return lax.broadcasted_iota(jnp.int32, (rc, tq), 1) < row

        def rows(ref, g):
            return ref[pl.ds(c0 * SB_HEADS + h0 + g, tq, stride=SB_HEADS), :].astype(BF16)

        def stages(g):
            def scores():
                z_ref[g] = _dot_nt(q_ref[:, g * D:(g + 1) * D], rows(k_ref, g)) * scale

            def split(r0):
                sp = _softplus_fast(z_ref[g, r0:r0 + rc, :])
                if masked:
                    sp = jnp.where(valid(r0), sp, 0.0)
                hi, lo = _split2(sp)
                hi_ref[g, r0:r0 + rc, :] = hi
                lo_ref[g, r0:r0 + rc, :] = lo

            def suffix(s):
                sl = slice(s * SB_SUB, (s + 1) * SB_SUB)
                cs = _dot(hi_ref[g, :, sl], tri) + _dot(lo_ref[g, :, sl], tri)
                carry = carry_ref[g]
                cs_ref[g, :, sl] = cs + carry
                carry_ref[g] = carry + cs[:, 0:1]

            def weights(r0):
                w = jnp.exp(z_ref[g, r0:r0 + rc, :] - cs_ref[g, r0:r0 + rc, :])
                if masked:
                    w = jnp.where(valid(r0), w, 0.0)
                w_ref[g, r0:r0 + rc, :] = w.astype(BF16)

            def values():
                acc_ref[g] += _dot(w_ref[g], rows(v_ref, g))

            part = functools.partial
            return [[scores],
                    [part(split, r0) for r0 in range(0, tq, rc)],
                    [part(suffix, s) for s in reversed(range(tq // SB_SUB))],
                    [part(weights, r0) for r0 in range(0, tq, rc)],
                    [values]]

        def emit(*lists):
            n = max(len(lst) for lst in lists)
            for t in range(n):
                for lst in lists:
                    for f in lst[t * len(lst) // n:(t + 1) * len(lst) // n]:
                        f()

        per_head = [stages(g) for g in range(hg)]
        for t in range(5 + hg - 1):
            emit(*[per_head[g][t - g] for g in range(hg) if 0 <= t - g < 5])

    chunk(qi * tq, True)

    def body(it, _):
        chunk((qi - 1 - it) * tq, False)
        return 0

    lax.fori_loop(0, qi, body, 0)
    for g in range(hg):
        o_ref[:, g * D:(g + 1) * D] = acc_ref[g].astype(o_ref.dtype)


def sb_attention_prompt(q, k, v, B, L, tq=512, hg=4):
    assert L % tq == 0 and tq % SB_SUB == 0 and SB_HEADS % hg == 0 and tq % CHUNK_ROWS == 0
    nq = L // tq
    kv = lambda: pl.BlockSpec((L * SB_HEADS, SB_HEAD_DIM), lambda b, h, i: (b, 0),
                              pipeline_mode=pl.Buffered(1))
    return pl.pallas_call(
        functools.partial(_sb_prompt_kernel, tq=tq, hg=hg),
        grid=(B, SB_HEADS // hg, nq),
        in_specs=[pl.BlockSpec((tq, hg * SB_HEAD_DIM), lambda b, h, i: (b * nq + i, h)), kv(), kv()],
        out_specs=pl.BlockSpec((tq, hg * SB_HEAD_DIM), lambda b, h, i: (b * nq + i, h)),
        out_shape=jax.ShapeDtypeStruct((B * L, SB_HEADS * SB_HEAD_DIM), BF16),
        scratch_shapes=[pltpu.VMEM((hg, tq, SB_HEAD_DIM), F32), pltpu.VMEM((hg, tq, 1), F32),
                        pltpu.VMEM((hg, tq, tq), F32), pltpu.VMEM((hg, tq, tq), F32),
                        pltpu.VMEM((hg, tq, tq), BF16), pltpu.VMEM((hg, tq, tq), BF16),
                        pltpu.VMEM((hg, tq, tq), BF16)],
        compiler_params=_params("parallel", "parallel", "arbitrary"),
        name="sb_attention_prompt",
    )(q, k, v)


def _sb_decode_kernel(qbd_ref, kn_ref, vn_ref, pk_ref, pv_ref, o_ref, knew_ref, vnew_ref, acc_ref,
                      *, ls, npast):
    H, D = SB_HEADS, SB_HEAD_DIM
    W = H * ls
    qbd = qbd_ref[...]
    scale = D ** -0.5
    acc_ref[...] = jnp.zeros_like(acc_ref)

    def block(k_row, v_row, nb, masked, carry):
        zt = _dot(k_row, qbd) * scale
        sp = _softplus_fast(zt)
        if masked:
            j = lax.broadcasted_iota(jnp.int32, (nb, W), 0)
            t = jnp.bitwise_and(lax.broadcasted_iota(jnp.int32, (nb, W), 1), ls - 1)
            valid = j < t
            sp = jnp.where(valid, sp, 0.0)
        rs = lax.broadcasted_iota(jnp.int32, (nb, nb), 0)
        rj = lax.broadcasted_iota(jnp.int32, (nb, nb), 1)
        tri_t = jnp.where(rj >= rs, 1.0, 0.0).astype(BF16)
        hi, lo = _split2(sp)
        cs = _dot(tri_t, hi) + _dot(tri_t, lo)
        w = jnp.exp(zt - cs - carry)
        if masked:
            w = jnp.where(valid, w, 0.0)
        acc_ref[...] += _dot(w.T.astype(BF16), v_row)
        return carry + cs[0:1, :]

    def rows(ref, k0, n):
        return jnp.concatenate(
            [ref[pl.ds(k0 * H + h, n, stride=H), :] for h in range(H)], axis=1).astype(BF16)

    knew_ref[...] = jnp.zeros_like(knew_ref)
    vnew_ref[...] = jnp.zeros_like(vnew_ref)
    knew_ref[0:ls, :] = rows(kn_ref, 0, ls)
    vnew_ref[0:ls, :] = rows(vn_ref, 0, ls)
    carry = block(knew_ref[...], vnew_ref[...], LANES, True, jnp.zeros((1, W), F32))

    def body(it, carry):
        k0 = pl.multiple_of((npast - 1 - it) * SB_SUB, SB_SUB)
        return block(rows(pk_ref, k0, SB_SUB), rows(pv_ref, k0, SB_SUB), SB_SUB, False, carry)

    lax.fori_loop(0, npast, body, carry)
    for h in range(H):
        o_ref[:, h * D:(h + 1) * D] = acc_ref[h * ls:(h + 1) * ls, h * D:(h + 1) * D].astype(o_ref.dtype)


def sb_attention_decode(q, kn, vn, cache_k, cache_v, layer, B, ls):
    H, D = SB_HEADS, SB_HEAD_DIM
    depth, _, P = cache_k.shape[:3]
    W = H * ls
    assert P % SB_SUB == 0 and ls <= LANES and ls & (ls - 1) == 0 and ls % SUBLANES == 0
    qbd = jnp.einsum("bthd,hg->bhdgt", q.reshape(B, ls, H, D), jnp.eye(H, dtype=q.dtype))
    qbd = qbd.reshape(B, H * D, W)
    row = pl.BlockSpec((ls, H * D), lambda b: (b, 0))
    new = pl.BlockSpec((ls * H, D), lambda b: (b, 0))
    past = pl.BlockSpec((None, None, P * H, D), lambda b: (layer, b, 0, 0))
    return pl.pallas_call(
        functools.partial(_sb_decode_kernel, ls=ls, npast=P // SB_SUB),
        grid=(B,),
        in_specs=[pl.BlockSpec((None, H * D, W), lambda b: (b, 0, 0)), new, new, past, past],
        out_specs=row,
        out_shape=jax.ShapeDtypeStruct((B * ls, H * D), BF16),
        scratch_shapes=[pltpu.VMEM((LANES, H * D), BF16), pltpu.VMEM((LANES, H * D), BF16),
                        pltpu.VMEM((W, H * D), F32)],
        compiler_params=_params("parallel"),
        name="sb_attention_decode",
    )(qbd, kn, vn, cache_k.reshape(depth, B, P * H, D), cache_v.reshape(depth, B, P * H, D))


def _conv_taps(xp_ref, x3, halo, K, sl):
    xp_ref[:, SUBLANES - (K - 1):SUBLANES, :] = halo
    xp_ref[:, SUBLANES:SUBLANES + sl, :] = x3
    base = SUBLANES - (K - 1)
    return [xp_ref[:, base + k:base + k + sl, :] for k in range(K)]


def conv_halos(x, state, B, L, tile):
    C = x.shape[1]
    km1 = state.shape[1]
    nt = L // tile
    if nt == 1:
        return state[:, None]
    tails = x.reshape(B, nt, tile, C)[:, :-1, tile - km1:, :]
    return jnp.concatenate([state[:, None], tails], axis=1)


def _lru_kernel(lx_ref, lg_ref, halo_ref, h0_ref, cw_ref, cb_ref, wa_ref, ba_ref, wx_ref, bx_ref,
                lam_ref, o_ref, hl_ref, xp_ref, a_ref, u_ref, h_ref, hc_ref, *, tl, nl):
    i = pl.program_id(1)
    W = lx_ref.shape[1]

    @pl.when(i == 0)
    def _():
        hc_ref[...] = h0_ref[...]

    taps = _conv_taps(xp_ref, lx_ref[...][None], halo_ref[...][None], 4, tl)
    xc = cb_ref[...]
    for k in range(4):
        xc = xc + cw_ref[k:k + 1, :] * taps[k][0]

    bd = W // LRU_BLOCKS
    for n in range(LRU_BLOCKS):
        sl = slice(n * bd, (n + 1) * bd)
        xn = xc[:, sl]
        xb = xn.astype(BF16)
        r = jax.nn.sigmoid(_dot(xb, wa_ref[n].astype(BF16)) + ba_ref[:, sl])
        ig = jax.nn.sigmoid(_dot(xb, wx_ref[n].astype(BF16)) + bx_ref[:, sl])
        log_a = LRU_C * r * (-_softplus(-lam_ref[:, sl]))
        a = jnp.exp(log_a)
        u = jnp.sqrt(-jnp.tanh(log_a) * (a * a + 1.0)) * (ig * xn)
        a_ref[:, sl] = a
        u_ref[:, sl] = u

    def step(t, h):
        h = a_ref[pl.ds(t, 1), :] * h + u_ref[pl.ds(t, 1), :]
        h_ref[pl.ds(t, 1), :] = h
        return h

    h = lax.fori_loop(0, tl, step, hc_ref[...], unroll=8)
    hc_ref[...] = h
    o_ref[...] = (h_ref[...] * _gelu_tanh(lg_ref[...])).astype(o_ref.dtype)

    @pl.when(i == nl - 1)
    def _():
        hl_ref[...] = h


def lru_branch(lx, lg, state, h0, cw, cb, wa, ba, wx, bx, lam, B, L, tl=256):
    W = lx.shape[1]
    tl = min(tl, L)
    nl = L // tl
    halo = conv_halos(lx, state, B, L, tl)
    kern = functools.partial(_lru_kernel, tl=tl, nl=nl)
    vec = lambda: pl.BlockSpec((1, W), lambda b, i: (0, 0))
    blk = lambda: pl.BlockSpec(wa.shape, lambda b, i: (0, 0, 0))
    o, hl = pl.pallas_call(
        kern,
        grid=(B, nl),
        in_specs=[
            pl.BlockSpec((tl, W), lambda b, i: (b * nl + i, 0)),
            pl.BlockSpec((tl, W), lambda b, i: (b * nl + i, 0)),
            pl.BlockSpec((None, None, 3, W), lambda b, i: (b, i, 0, 0)),
            pl.BlockSpec((None, 1, W), lambda b, i: (b, 0, 0)),
            pl.BlockSpec((4, W), lambda b, i: (0, 0)),
            vec(), blk(), vec(), blk(), vec(), vec(),
        ],
        out_specs=[
            pl.BlockSpec((tl, W), lambda b, i: (b * nl + i, 0)),
            pl.BlockSpec((None, 1, W), lambda b, i: (b, 0, 0)),
        ],
        out_shape=[
            jax.ShapeDtypeStruct((B * L, W), BF16),
            jax.ShapeDtypeStruct((B, 1, W), F32),
        ],
        scratch_shapes=[
            pltpu.VMEM((1, tl + SUBLANES, W), F32),
            pltpu.VMEM((tl, W), F32),
            pltpu.VMEM((tl, W), F32),
            pltpu.VMEM((tl, W), F32),
            pltpu.VMEM((1, W), F32),
        ],
        compiler_params=_params("parallel", "arbitrary"),
        name="lru_branch",
    )(lx, lg, halo, h0.reshape(B, 1, W), cw, cb.reshape(1, W), wa, ba.reshape(1, W),
      wx, bx.reshape(1, W), lam.reshape(1, W))
    return o, hl.reshape(B, W)


def _ssd_kernel(z_ref, xbc_ref, dt_ref, halo_ref, h0_ref, cw_ref, cb_ref, dtb_ref, alog_ref,
                dexp_ref, nw_ref, o_ref, hl_ref, xp_ref, y_ref, st_ref, *, q, nc, n_valid):
    c = pl.program_id(1)
    inner = SSD_HEADS * SSD_HEAD_DIM
    gw = inner // SSD_GROUPS
    epg = SSD_HEADS // SSD_GROUPS
    n = SSD_STATE

    @pl.when(c == 0)
    def _():
        st_ref[...] = h0_ref[...]

    taps = _conv_taps(xp_ref, xbc_ref[...][None], halo_ref[...][None], 4, q)
    xc = cb_ref[...]
    for k in range(4):
        xc = xc + cw_ref[k:k + 1, :] * taps[k][0]
    xc = _silu(xc)
    xs = xc[:, :inner]
    bm = xc[:, inner:inner + SSD_GROUPS * n]
    cm = xc[:, inner + SSD_GROUPS * n:]

    rowi = lax.broadcasted_iota(jnp.int32, (q, LANES), 0)
    lanei = lax.broadcasted_iota(jnp.int32, (q, LANES), 1)
    dts = jnp.where(lanei < SSD_HEADS, _softplus(dt_ref[...] + dtb_ref[...]), 0.0)
    if n_valid < q:
        dts = jnp.where(rowi < n_valid, dts, 0.0)
    dta = dts * (-jnp.exp(alog_ref[...]))

    ti = lax.broadcasted_iota(jnp.int32, (q, q), 0)
    si = lax.broadcasted_iota(jnp.int32, (q, q), 1)
    causal = si <= ti
    lower = jnp.where(causal, 1.0, 0.0).astype(BF16)
    cum = sum(_dot(lower, p) for p in _split3(dta))
    cum_t = cum.T
    cum_last = cum[q - 1:q, :]
    exp_cum = jnp.exp(cum)
    decay_end = jnp.exp(cum_last - cum)

    eh = lax.broadcasted_iota(jnp.int32, (LANES, inner), 0)
    ec = lax.broadcasted_iota(jnp.int32, (LANES, inner), 1)
    expand = jnp.where(ec // SSD_HEAD_DIM == eh, 1.0, 0.0).astype(BF16)

    def expand_heads(a):
        return sum(_dot(p, expand) for p in _split3(a))

    dts_e = expand_heads(dts)
    exp_cum_e = expand_heads(exp_cum)
    decay_end_e = expand_heads(decay_end)
    chunk_decay_e = exp_cum_e[q - 1:q, :]

    xdt = xs * dts_e
    xdt_b = xdt.astype(BF16)
    xend_b = (decay_end_e * xdt).astype(BF16)

    for g in range(SSD_GROUPS):
        bg = bm[:, g * n:(g + 1) * n]
        cg_b = cm[:, g * n:(g + 1) * n].astype(BF16)
        gs = slice(g * gw, (g + 1) * gw)
        cb = _dot_nt(cg_b, bg.astype(BF16))
        st = st_ref[g]
        y_inter = _dot(cg_b, st.astype(BF16)) * exp_cum_e[:, gs]
        for e in range(epg):
            h = g * epg + e
            hs = slice(h * SSD_HEAD_DIM, (h + 1) * SSD_HEAD_DIM)
            seg = cum[:, h:h + 1] - cum_t[h:h + 1, :]
            decay = jnp.where(causal, jnp.exp(seg), 0.0)
            wts = (cb * decay).astype(BF16)
            y_ref[:, hs] = _dot(wts, xdt_b[:, hs]) + y_inter[:, e * SSD_HEAD_DIM:(e + 1) * SSD_HEAD_DIM]
        bg_t = bg.T.astype(BF16)
        st_ref[g] = chunk_decay_e[:, gs] * st + _dot(bg_t, xend_b[:, gs])

    y = y_ref[...] + dexp_ref[...] * xs
    yg = y * _silu(z_ref[...])
    for g in range(SSD_GROUPS):
        gs = slice(g * gw, (g + 1) * gw)
        part = yg[:, gs]
        ms = jnp.mean(part * part, axis=-1, keepdims=True)
        o_ref[:, gs] = (part * lax.rsqrt(ms + NORM_EPS) * nw_ref[:, gs]).astype(o_ref.dtype)

    @pl.when(c == nc - 1)
    def _():
        hl_ref[...] = st_ref[...]


def _pad_rows(x, B, L, Lp):
    if Lp == L:
        return x
    C = x.shape[1]
    return jnp.pad(x.reshape(B, L, C), ((0, 0), (0, Lp - L), (0, 0))).reshape(B * Lp, C)


def ssd_branch(z, xbc, dt, dt_block, state, h0, cw, cb, dt_bias, a_log, d, norm_w, B, L, q=128):
    inner = z.shape[1]
    cd = xbc.shape[1]
    nh = SSD_HEADS
    Lp = max(L, q) if L % q else L
    n_valid = min(L, q)
    nc = Lp // q
    gw = inner // SSD_GROUPS
    epg = SSD_HEADS // SSD_GROUPS
    halo = conv_halos(xbc, state, B, L, q) if L >= q else state[:, None]
    zp = _pad_rows(z, B, L, Lp)
    xbcp = _pad_rows(xbc, B, L, Lp)
    if Lp != L:
        dt, dt_block = dt[:, dt_block * LANES:(dt_block + 1) * LANES], 0
    dtp = _pad_rows(dt, B, L, Lp)
    h0t = h0.reshape(B, SSD_GROUPS, epg, SSD_HEAD_DIM, SSD_STATE).transpose(0, 1, 4, 2, 3)
    h0t = h0t.reshape(B, SSD_GROUPS, SSD_STATE, gw)
    pad1 = lambda a: jnp.pad(a, (0, LANES - nh)).reshape(1, LANES)
    kern = functools.partial(_ssd_kernel, q=q, nc=nc, n_valid=n_valid)
    row = lambda w: pl.BlockSpec((q, w), lambda b, c: (b * nc + c, 0))
    vec = lambda w: pl.BlockSpec((1, w), lambda b, c: (0, 0))
    o, hl = pl.pallas_call(
        kern,
        grid=(B, nc),
        in_specs=[
            row(inner), row(cd), pl.BlockSpec((q, LANES), lambda b, c: (b * nc + c, dt_block)),
            pl.BlockSpec((None, None, 3, cd), lambda b, c: (b, c, 0, 0)),
            pl.BlockSpec((None, SSD_GROUPS, SSD_STATE, gw), lambda b, c: (b, 0, 0, 0)),
            pl.BlockSpec((4, cd), lambda b, c: (0, 0)),
            vec(cd), vec(LANES), vec(LANES), vec(inner), vec(inner),
        ],
        out_specs=[
            row(inner),
            pl.BlockSpec((None, SSD_GROUPS, SSD_STATE, gw), lambda b, c: (b, 0, 0, 0)),
        ],
        out_shape=[
            jax.ShapeDtypeStruct((B * Lp, inner), BF16),
            jax.ShapeDtypeStruct((B, SSD_GROUPS, SSD_STATE, gw), F32),
        ],
        scratch_shapes=[
            pltpu.VMEM((1, q + SUBLANES, cd), F32),
            pltpu.VMEM((q, inner), F32),
            pltpu.VMEM((SSD_GROUPS, SSD_STATE, gw), F32),
        ],
        compiler_params=_params("parallel", "arbitrary"),
        name="ssd_branch",
    )(zp, xbcp, dtp, halo, h0t, cw, cb.reshape(1, cd), pad1(dt_bias), pad1(a_log),
      jnp.repeat(d, SSD_HEAD_DIM).reshape(1, inner), norm_w.reshape(1, inner))
    if Lp != L:
        o = o.reshape(B, Lp, inner)[:, :L].reshape(B * L, inner)
    hl = hl.reshape(B, SSD_GROUPS, SSD_STATE, epg, SSD_HEAD_DIM).transpose(0, 1, 3, 4, 2)
    return o, hl.reshape(B, SSD_HEADS, SSD_HEAD_DIM, SSD_STATE)


def _merge_kernel(x_ref, g_ref, a_ref, b_ref, c_ref, wg0_ref, wg1_ref, wg2_ref, wb_ref, o_ref, xn_ref):
    @pl.when(pl.program_id(1) == 0)
    def _():
        x = x_ref[...]
        ms = jnp.mean(x * x, axis=-1, keepdims=True)
        xn_ref[...] = (x * lax.rsqrt(ms + NORM_EPS) * g_ref[...]).astype(BF16)

    xn = xn_ref[...]
    acc = None
    for n, (br, wg) in enumerate(((a_ref, wg0_ref), (b_ref, wg1_ref), (c_ref, wg2_ref))):
        gate = _dot_nt(xn, wg[...].astype(BF16))
        t = _dot(br[...], wb_ref[n].astype(BF16)) * jax.nn.sigmoid(gate)
        acc = t if acc is None else acc + t
    o_ref[...] = acc.astype(o_ref.dtype)


def gated_merge(x, g, oa, ob, oc, w_gate_t, wb, layer, tm=1024, tn=256):
    T, K = oa.shape
    D = x.shape[1]
    N = wb.shape[-1]
    tm = min(tm, T)
    nj = N // tn
    br = lambda: pl.BlockSpec((tm, K), lambda i, j: (i, 0))
    gate = lambda n: pl.BlockSpec((None, tn, D), lambda i, j: (layer, n * nj + j, 0))
    return pl.pallas_call(
        _merge_kernel,
        grid=(T // tm, nj),
        in_specs=[pl.BlockSpec((tm, D), lambda i, j: (i, 0), pipeline_mode=pl.Buffered(1)),
                  pl.BlockSpec((1, D), lambda i, j: (0, 0)),
                  br(), br(), br(), gate(0), gate(1), gate(2),
                  pl.BlockSpec((None, 3, K, tn), lambda i, j: (layer, 0, 0, j))],
        out_specs=pl.BlockSpec((tm, tn), lambda i, j: (i, j)),
        out_shape=jax.ShapeDtypeStruct((T, N), BF16),
        scratch_shapes=[pltpu.VMEM((tm, D), BF16)],
        compiler_params=_params("parallel", "arbitrary"),
        name="gated_merge",
    )(x, g.reshape(1, D), oa, ob, oc, w_gate_t, w_gate_t, w_gate_t, wb)


def _xattn_kernel(q_ref, mk_ref, mv_ref, o_ref):
    hd = q_ref.shape[1] // XA_HEADS
    scale = hd ** -0.5
    for h in range(XA_HEADS):
        sl = slice(h * hd, (h + 1) * hd)
        kh, vh = (mk_ref[:, sl], mv_ref[:, sl]) if len(mk_ref.shape) == 2 else (mk_ref[:, h, :], mv_ref[:, h, :])
        s = _dot_nt(q_ref[:, sl].astype(BF16), kh.astype(BF16)) * scale
        m = jnp.max(s, axis=-1, keepdims=True)
        p = jnp.exp(s - m)
        p = p / jnp.sum(p, axis=-1, keepdims=True)
        o_ref[:, sl] = _dot(p.astype(BF16), vh.astype(BF16)).astype(o_ref.dtype)


def cross_attention(q, mk, mv, layer, B, L, tq=512):
    D = q.shape[1]
    tq = min(tq, L)
    nq = L // tq
    if layer is None:
        mem = lambda: pl.BlockSpec((None,) + mk.shape[1:], lambda b, i: (b, 0, 0))
    else:
        mem = lambda: pl.BlockSpec((None, None) + mk.shape[2:], lambda b, i: (layer, b, 0, 0, 0))
    return pl.pallas_call(
        _xattn_kernel,
        grid=(B, nq),
        in_specs=[pl.BlockSpec((tq, D), lambda b, i: (b * nq + i, 0)), mem(), mem()],
        out_specs=pl.BlockSpec((tq, D), lambda b, i: (b * nq + i, 0)),
        out_shape=jax.ShapeDtypeStruct((B * L, D), BF16),
        compiler_params=_params("parallel", "arbitrary"),
        name="cross_attention",
    )(q, mk, mv)


def _mem_kv_kernel(x_ref, g_ref, wk_ref, wv_ref, k_ref, v_ref, k2_ref, v2_ref, xn_ref):
    j = pl.program_id(1)

    @pl.when(j == 0)
    def _():
        x = x_ref[...]
        ms = jnp.mean(x * x, axis=-1, keepdims=True)
        xn_ref[...] = (x * lax.rsqrt(ms + NORM_EPS) * g_ref[...]).astype(BF16)

    xn = xn_ref[...]
    k = _dot(xn, wk_ref[...].astype(BF16))
    v = _dot(xn, wv_ref[...].astype(BF16))
    k2_ref[...] = k.astype(k2_ref.dtype)
    v2_ref[...] = v.astype(v2_ref.dtype)
    for h in range(XA_HEADS):
        @pl.when(j == h)
        def _(h=h):
            k_ref[:, h, :] = k
            v_ref[:, h, :] = v


def mem_kv(mem, g, wk, wv, layer, B, M):
    D = mem.shape[1]
    hd = D // XA_HEADS
    w = lambda: pl.BlockSpec((None, D, hd), lambda b, j: (layer, 0, j))
    o4 = lambda: pl.BlockSpec((None, M, XA_HEADS, hd), lambda b, j: (b, 0, 0, 0))
    o2 = lambda: pl.BlockSpec((None, M, hd), lambda b, j: (b, 0, j))
    return pl.pallas_call(
        _mem_kv_kernel,
        grid=(B, XA_HEADS),
        in_specs=[pl.BlockSpec((M, D), lambda b, j: (b, 0)), pl.BlockSpec((1, D), lambda b, j: (0, 0)),
                  w(), w()],
        out_specs=[o4(), o4(), o2(), o2()],
        out_shape=[jax.ShapeDtypeStruct((B, M, XA_HEADS, hd), F32)] * 2
        + [jax.ShapeDtypeStruct((B, M, D), BF16)] * 2,
        scratch_shapes=[pltpu.VMEM((M, D), BF16)],
        compiler_params=_params("parallel", "arbitrary"),
        name="mem_kv",
    )(mem, g.reshape(1, D), wk, wv)


def _ffn_up_kernel(x_ref, g_ref, wg_ref, wv_ref, sg_ref, sv_ref, cwg_ref, cwv_ref, cbg_ref, cbv_ref,
                   act_ref, tg_ref, tv_ref, xn_ref, xp_ref, carry_ref, *, nseg, sl, tpb):
    i = pl.program_id(0)
    f = pl.program_id(1)
    tm, tf = act_ref.shape

    @pl.when(f == 0)
    def _():
        x = x_ref[...]
        ms = jnp.mean(x * x, axis=-1, keepdims=True)
        xn_ref[...] = (x * lax.rsqrt(ms + NORM_EPS) * g_ref[...]).astype(BF16)

    if tpb > 1:
        @pl.when((i == 0) & (f == 0))
        def _():
            carry_ref[...] = jnp.zeros_like(carry_ref)

    xn = xn_ref[...]
    base = SUBLANES - 2
    starts = i % tpb == 0
    ts = xp_ref.shape[-1]

    def half(s, c, w_ref, st_ref, cw_ref, cb_ref, tail_ref):
        cols = slice(s * ts, (s + 1) * ts)
        xp = xp_ref.at[2 * s + c]
        u3 = _dot(xn, w_ref[:, cols].astype(BF16)).reshape(nseg, sl, ts)
        halo = st_ref[:, :, cols]
        if tpb > 1:
            halo = jnp.where(starts, halo, carry_ref[c, f, :, :, cols])
        xp[:, base:SUBLANES, :] = halo
        xp[:, SUBLANES:SUBLANES + sl, :] = u3
        y = cb_ref[:, cols]
        for k in range(3):
            y = y + cw_ref[k:k + 1, cols] * xp[:, base + k:base + k + sl, :].reshape(tm, ts)
        tail = xp[:, SUBLANES + sl - 2:SUBLANES + sl, :]
        if tpb > 1:
            carry_ref[c, f, :, :, cols] = tail
        tail_ref[:, :, cols] = tail
        return y

    for s in range(tf // ts):
        gate = half(s, 0, wg_ref, sg_ref, cwg_ref, cbg_ref, tg_ref)
        val = half(s, 1, wv_ref, sv_ref, cwv_ref, cbv_ref, tv_ref)
        act_ref[:, s * ts:(s + 1) * ts] = (_silu(gate) * val).astype(act_ref.dtype)


def ffn_up(x, g, w_up, layer, state, cw, cb, B, L, tm=1024, tf=512):
    T, D = x.shape
    F2 = w_up.shape[-1]
    F = F2 // 2
    if L >= tm:
        nseg, sl, tpb = 1, tm, L // tm
    else:
        nseg, sl, tpb = min(tm // L, B), L, 1
        tm = nseg * sl
    nt, nf = T // tm, F // tf
    ts = min(tf, MXU_WIDTH)
    st = state.reshape(B // nseg, nseg, 2, F2)
    cb2 = cb.reshape(1, F2)
    kern = functools.partial(_ffn_up_kernel, nseg=nseg, sl=sl, tpb=tpb)
    wspec = lambda off: pl.BlockSpec((None, D, tf), lambda i, f: (layer, 0, off + f))
    sspec = lambda off: pl.BlockSpec((None, nseg, 2, tf), lambda i, f: (i // tpb, 0, 0, off + f))
    cwspec = lambda off: pl.BlockSpec((3, tf), lambda i, f: (0, off + f))
    cbspec = lambda off: pl.BlockSpec((1, tf), lambda i, f: (0, off + f))
    tspec = lambda: pl.BlockSpec((None, nseg, 2, tf), lambda i, f: (i, 0, 0, f))
    act, tg, tv = pl.pallas_call(
        kern,
        grid=(nt, nf),
        in_specs=[
            pl.BlockSpec((tm, D), lambda i, f: (i, 0), pipeline_mode=pl.Buffered(1)),
            pl.BlockSpec((1, D), lambda i, f: (0, 0)),
            wspec(0), wspec(nf), sspec(0), sspec(nf), cwspec(0), cwspec(nf), cbspec(0), cbspec(nf),
        ],
        out_specs=[pl.BlockSpec((tm, tf), lambda i, f: (i, f)), tspec(), tspec()],
        out_shape=[
            jax.ShapeDtypeStruct((T, F), BF16),
            jax.ShapeDtypeStruct((nt, nseg, 2, F), F32),
            jax.ShapeDtypeStruct((nt, nseg, 2, F), F32),
        ],
        scratch_shapes=[
            pltpu.VMEM((tm, D), BF16),
            pltpu.VMEM((2 * (tf // ts), nseg, sl + SUBLANES, ts), F32),
            pltpu.VMEM((2, nf, nseg, 2, tf), F32),
        ],
        compiler_params=_params("arbitrary", "arbitrary"),
        name="ffn_up",
    )(x, g.reshape(1, D), w_up, w_up, st, st, cw, cw, cb2, cb2)
    last = lambda t: t.reshape(B // nseg, tpb, nseg, 2, F)[:, -1].reshape(B, 2, F)
    return act, jnp.concatenate([last(tg), last(tv)], axis=-1)


def _last_rows(x, state, B, L):
    km1 = state.shape[1]
    x3 = x.reshape(B, L, x.shape[1])
    if L >= km1:
        return x3[:, L - km1:]
    return jnp.concatenate([state, x3], axis=1)[:, -km1:]


def trunk_layer(x, B, L, mk, mv, mem_layer, past_k, past_v, lru_buf, lru_h, ssd_buf, ssd_h, ffn_buf, w, l):
    mix = SB_HEADS * SB_HEAD_DIM
    inner = SSD_HEADS * SSD_HEAD_DIM
    cd = inner + 2 * SSD_GROUPS * SSD_STATE
    D = x.shape[1]
    tn = 512
    q, k, v, lx, lg, z, xbc, dtx = in_proj(x, w["norm_mix"][l], w["w_in_t"], l,
                                           (mix,) * 5 + (inner, cd, tn), (BF16,) + (F32,) * 7,
                                           (False, True, True) + (False,) * 5, SB_HEAD_DIM, tn=tn)

    if past_k is None:
        o_a = sb_attention_prompt(q, k, v, B, L)
    else:
        o_a = sb_attention_decode(q, k, v, past_k, past_v, l, B, L)
    o_b, lru_h_new = lru_branch(lx, lg, lru_buf, lru_h, w["lru_conv_w"][l], w["lru_conv_b"][l], w["lru_wa"][l],
                                w["lru_ba"][l], w["lru_wx"][l], w["lru_bx"][l], w["lru_lam"][l], B, L)
    o_c, ssd_h_new = ssd_branch(z, xbc, dtx, 0, ssd_buf, ssd_h, w["ssd_conv_w"][l],
                                w["ssd_conv_b"][l], w["ssd_dt_bias"][l], w["ssd_a_log"][l], w["ssd_d"][l],
                                w["ssd_norm"][l], B, L)
    merged = gated_merge(x, w["norm_mix"][l], o_a, o_b, o_c, w["w_gate_t"], w["w_branch"], l)
    h = matmul_res(merged, w["w_out"], l, x, tm=2048)
    qx = norm_matmul(h, w["norm_xattn"][l], w["w_xq"], l, out_dtype=BF16)
    o_x = cross_attention(qx, mk, mv, mem_layer, B, L)
    h = matmul_res(o_x, w["w_xo"], l, h, tm=2048)
    act, ffn_new = ffn_up(h, w["norm_ffn"][l], w["w_up"], l, ffn_buf, w["ffn_conv_w"][l],
                          w["ffn_conv_b"][l], B, L)
    out = matmul_res(act, w["w_down"], l, h, tn=256)
    states = (k.reshape(B, L, SB_HEADS, SB_HEAD_DIM), v.reshape(B, L, SB_HEADS, SB_HEAD_DIM),
              _last_rows(lx, lru_buf, B, L), lru_h_new, _last_rows(xbc, ssd_buf, B, L), ssd_h_new,
              ffn_new)
    return out, states


def kernel(x_prompt, x_sample, mem_prompt, cache_sb_k, cache_sb_v, cache_mem_k, cache_mem_v, state_lru_conv, state_lru_h, state_ssd_conv, state_ssd, state_ffn_conv, norm_mix, w_in, lru_conv_w, lru_conv_b, lru_wa, lru_ba, lru_wx, lru_bx, lru_lam, ssd_conv_w, ssd_conv_b, ssd_dt_bias, ssd_a_log, ssd_d, ssd_norm, w_branch, w_out, norm_xattn, norm_mem, w_xq, w_xk, w_xv, w_xo, norm_ffn, w_up, ffn_conv_w, ffn_conv_b, w_down, norm_final):
    depth = w_in.shape[0]
    Bp, Lp, D = x_prompt.shape
    Bs, Ls, _ = x_sample.shape
    M = mem_prompt.shape[1]
    mix = SB_HEADS * SB_HEAD_DIM
    inner = SSD_HEADS * SSD_HEAD_DIM
    cd = inner + 2 * SSD_GROUPS * SSD_STATE
    f2 = w_up.shape[2]
    xa_hd = D // XA_HEADS

    yp = x_prompt.reshape(Bp * Lp, D)
    ys = x_sample.reshape(Bs * Ls, D)
    mem = mem_prompt.reshape(Bp * M, D)
    st_p, st_s, mk_list, mv_list = [], [], [], []
    w_in_t = jnp.swapaxes(w_in, 1, 2)
    w_gate_t = w_in_t[:, 5 * mix + inner + cd + SSD_HEADS:]
    w = dict(w_in_t=w_in_t, w_gate_t=w_gate_t, norm_mix=norm_mix, lru_conv_w=lru_conv_w, lru_conv_b=lru_conv_b,
             lru_wa=lru_wa, lru_ba=lru_ba, lru_wx=lru_wx, lru_bx=lru_bx, lru_lam=lru_lam,
             ssd_conv_w=ssd_conv_w, ssd_conv_b=ssd_conv_b, ssd_dt_bias=ssd_dt_bias, ssd_a_log=ssd_a_log,
             ssd_d=ssd_d, ssd_norm=ssd_norm, w_branch=w_branch, w_out=w_out, norm_xattn=norm_xattn,
             w_xq=w_xq, w_xo=w_xo, norm_ffn=norm_ffn, w_up=w_up, ffn_conv_w=ffn_conv_w,
             ffn_conv_b=ffn_conv_b, w_down=w_down)
    for l in range(depth):
        mk_p, mv_p, mk2, mv2 = mem_kv(mem, norm_mem[l], w_xk, w_xv, l, Bp, M)
        zeros = lambda *s: jnp.zeros(s, F32)
        yp, sp = trunk_layer(yp, Bp, Lp, mk2, mv2, None, None, None, zeros(Bp, 3, mix),
                             zeros(Bp, mix), zeros(Bp, 3, cd), zeros(Bp, SSD_HEADS, SSD_HEAD_DIM, SSD_STATE),
                             zeros(Bp, 2, f2), w, l)
        ys, ss = trunk_layer(ys, Bs, Ls, cache_mem_k, cache_mem_v, l,
                             cache_sb_k, cache_sb_v, state_lru_conv[l], state_lru_h[l],
                             state_ssd_conv[l], state_ssd[l], state_ffn_conv[l], w, l)
        st_p.append(sp)
        st_s.append(ss)
        mk_list.append(mk_p)
        mv_list.append(mv_p)
    y_prompt = rmsnorm(yp, norm_final).reshape(Bp, Lp, D)
    y_sample = rmsnorm(ys, norm_final).reshape(Bs, Ls, D)

    def stk(lst, i):
        return jnp.stack([s[i] for s in lst], axis=0)

    return (y_prompt, y_sample,
            stk(st_p, 0), stk(st_p, 1), stk(st_p, 2), stk(st_p, 3), stk(st_p, 4), stk(st_p, 5), stk(st_p, 6),
            jnp.stack(mk_list, axis=0), jnp.stack(mv_list, axis=0),
            stk(st_s, 0), stk(st_s, 1), stk(st_s, 2), stk(st_s, 3), stk(st_s, 4), stk(st_s, 5), stk(st_s, 6))
```

```python
import functools
import math

import jax
import jax.numpy as jnp
from jax import lax
from jax.experimental import pallas as pl
from jax.experimental.pallas import tpu as pltpu

F32 = jnp.float32
BF16 = jnp.bfloat16

NORM_EPS = 1e-6
LRU_C = 8.0
LANES = 128
SUBLANES = 8
MXU_WIDTH = 256
CHUNK_ROWS = 32
VMEM_LIMIT_BYTES = 56 * 1024 * 1024

SB_HEADS = 8
SB_HEAD_DIM = 128
LRU_BLOCKS = 8
SSD_HEADS = 16
SSD_HEAD_DIM = 64
SSD_GROUPS = 2
SSD_STATE = 128
XA_HEADS = 4


def _params(*sem):
    return pltpu.CompilerParams(dimension_semantics=sem, vmem_limit_bytes=VMEM_LIMIT_BYTES)


def _dot(a, b):
    return jnp.dot(a, b, preferred_element_type=F32)


def _dot_nt(a, b):
    return lax.dot_general(a, b, (((1,), (1,)), ((), ())), preferred_element_type=F32)


def _split3(x):
    hi = x.astype(BF16)
    r = x - hi.astype(F32)
    mid = r.astype(BF16)
    lo = (r - mid.astype(F32)).astype(BF16)
    return hi, mid, lo


def _softplus(x):
    return jnp.maximum(x, 0.0) + jnp.log1p(jnp.exp(-jnp.abs(x)))


def _silu(x):
    return x * jax.nn.sigmoid(x)


def _gelu_tanh(x):
    c = math.sqrt(2.0 / math.pi)
    return 0.5 * x * (1.0 + jnp.tanh(c * (x + 0.044715 * (x * x * x))))


def _norm_matmul_kernel(x_ref, g_ref, w_ref, o_ref, xn_ref, *, transposed):
    @pl.when(pl.program_id(1) == 0)
    def _():
        x = x_ref[...]
        ms = jnp.mean(x * x, axis=-1, keepdims=True)
        xn_ref[...] = (x * lax.rsqrt(ms + NORM_EPS) * g_ref[...]).astype(BF16)

    w = w_ref[...].astype(BF16)
    acc = _dot_nt(xn_ref[...], w) if transposed else _dot(xn_ref[...], w)
    o_ref[...] = acc.astype(o_ref.dtype)


def _layer_spec(w, layer, block, index):
    if layer is None:
        return pl.BlockSpec(block, index)
    return pl.BlockSpec((None,) + block, lambda *ids: (layer,) + index(*ids))


def norm_matmul(x, g, w, layer=None, out_dtype=F32, transposed=False, tm=1024, tn=512):
    T, D = x.shape
    N = w.shape[-2] if transposed else w.shape[-1]
    tm = min(tm, T)
    tn = min(tn, N)
    w_spec = (_layer_spec(w, layer, (tn, D), lambda i, j: (j, 0)) if transposed
              else _layer_spec(w, layer, (D, tn), lambda i, j: (0, j)))
    return pl.pallas_call(
        functools.partial(_norm_matmul_kernel, transposed=transposed),
        grid=(T // tm, N // tn),
        in_specs=[
            pl.BlockSpec((tm, D), lambda i, j: (i, 0)),
            pl.BlockSpec((1, D), lambda i, j: (0, 0)),
            w_spec,
        ],
        out_specs=pl.BlockSpec((tm, tn), lambda i, j: (i, j)),
        out_shape=jax.ShapeDtypeStruct((T, N), out_dtype),
        scratch_shapes=[pltpu.VMEM((tm, D), BF16)],
        compiler_params=_params("parallel", "arbitrary"),
        name="norm_matmul",
    )(x, g.reshape(1, D), w)


def _in_proj_kernel(x_ref, g_ref, w_ref, *refs, ranges, by_head, hd):
    out_refs, xn_ref = refs[:-1], refs[-1]
    j = pl.program_id(1)
    tm, tn = xn_ref.shape[0], w_ref.shape[0]

    @pl.when(j == 0)
    def _():
        x = x_ref[...]
        ms = jnp.mean(x * x, axis=-1, keepdims=True)
        xn_ref[...] = (x * lax.rsqrt(ms + NORM_EPS) * g_ref[...]).astype(BF16)

    acc = _dot_nt(xn_ref[...], w_ref[...].astype(BF16))
    for o_ref, (lo, hi), heads in zip(out_refs, ranges, by_head):
        @pl.when((j >= lo) & (j < hi))
        def _(o_ref=o_ref, lo=lo, heads=heads):
            if heads:
                nh = (hi - lo) * (tn // hd)
                for c in range(tn // hd):
                    o_ref[pl.ds((j - lo) * (tn // hd) + c, tm, stride=nh), :] = acc[:, c * hd:(c + 1) * hd]
            else:
                o_ref[...] = acc.astype(o_ref.dtype)


def in_proj(x, g, w_t, layer, widths, dtypes, by_head, hd, tm=1024, tn=512):
    T, D = x.shape
    tm = min(tm, T)
    ranges, nblocks = [], 0
    for wd in widths:
        assert wd % tn == 0
        ranges.append((nblocks, nblocks + wd // tn))
        nblocks += wd // tn
    out_specs, out_shape = [], []
    for (lo, hi), wd, dt, heads in zip(ranges, widths, dtypes, by_head):
        if heads:
            out_specs.append(pl.BlockSpec((tm * (wd // hd), hd), lambda i, j: (i, 0),
                                          pipeline_mode=pl.Buffered(1)))
            out_shape.append(jax.ShapeDtypeStruct((T * (wd // hd), hd), dt))
        else:
            out_specs.append(pl.BlockSpec(
                (tm, tn), lambda i, j, lo=lo, n=hi - lo: (i, jnp.clip(j - lo, 0, n - 1))))
            out_shape.append(jax.ShapeDtypeStruct((T, wd), dt))
    return pl.pallas_call(
        functools.partial(_in_proj_kernel, ranges=tuple(ranges), by_head=tuple(by_head), hd=hd),
        grid=(T // tm, nblocks),
        in_specs=[
            pl.BlockSpec((tm, D), lambda i, j: (i, 0)),
            pl.BlockSpec((1, D), lambda i, j: (0, 0)),
            pl.BlockSpec((None, tn, D), lambda i, j: (layer, j, 0)),
        ],
        out_specs=out_specs,
        out_shape=out_shape,
        scratch_shapes=[pltpu.VMEM((tm, D), BF16)],
        compiler_params=_params("parallel", "arbitrary"),
        name="in_proj",
    )(x, g.reshape(1, D), w_t)


def _matmul_res_kernel(x_ref, w_ref, r_ref, o_ref):
    o_ref[...] = r_ref[...] + _dot(x_ref[...], w_ref[...].astype(BF16))


def matmul_res(x, w, layer, res, tm=1024, tn=512):
    T, K = x.shape
    N = w.shape[-1]
    tm = min(tm, T)
    tn = min(tn, N)
    return pl.pallas_call(
        _matmul_res_kernel,
        grid=(T // tm, N // tn),
        in_specs=[
            pl.BlockSpec((tm, K), lambda i, j: (i, 0)),
            _layer_spec(w, layer, (K, tn), lambda i, j: (0, j)),
            pl.BlockSpec((tm, tn), lambda i, j: (i, j)),
        ],
        out_specs=pl.BlockSpec((tm, tn), lambda i, j: (i, j)),
        out_shape=jax.ShapeDtypeStruct((T, N), F32),
        compiler_params=_params("parallel", "arbitrary"),
        name="matmul_res",
    )(x, w, res)


def _rmsnorm_kernel(x_ref, g_ref, o_ref):
    x = x_ref[...]
    ms = jnp.mean(x * x, axis=-1, keepdims=True)
    o_ref[...] = x * lax.rsqrt(ms + NORM_EPS) * g_ref[...]


def rmsnorm(x, g, tm=512):
    T, D = x.shape
    tm = min(tm, T)
    return pl.pallas_call(
        _rmsnorm_kernel,
        grid=(T // tm,),
        in_specs=[pl.BlockSpec((tm, D), lambda i: (i, 0)), pl.BlockSpec((1, D), lambda i: (0, 0))],
        out_specs=pl.BlockSpec((tm, D), lambda i: (i, 0)),
        out_shape=jax.ShapeDtypeStruct((T, D), F32),
        compiler_params=_params("parallel"),
        name="rmsnorm",
    )(x, g.reshape(1, D))


SB_SUB = MXU_WIDTH


def _softplus_fast(z):
    return jnp.maximum(z, 0.0) + jnp.log(1.0 + jnp.exp(-jnp.abs(z)))


def _split2(x):
    hi = x.astype(BF16)
    return hi, (x - hi.astype(F32)).astype(BF16)


def _sb_prompt_kernel(q_ref, k_ref, v_ref, o_ref, acc_ref, carry_ref, z_ref, cs_ref, hi_ref, lo_ref, w_ref,
                      *, tq, hg):
    h0 = pl.program_id(1) * hg
    qi = pl.program_id(2)
    D = SB_HEAD_DIM
    rj = lax.broadcasted_iota(jnp.int32, (SB_SUB, SB_SUB), 0)
    rs = lax.broadcasted_iota(jnp.int32, (SB_SUB, SB_SUB), 1)
    tri = jnp.where(rj >= rs, 1.0, 0.0).astype(BF16)
    scale = D ** -0.5
    acc_ref[...] = jnp.zeros_like(acc_ref)
    carry_ref[...] = jnp.zeros_like(carry_ref)

    def chunk(c0, masked):
        c0 = pl.multiple_of(c0, tq)
        rc = CHUNK_ROWS

        def valid(r0):
            row = r0 + lax.broadcasted_iota(jnp.int32, (rc, tq), 0)
            return lax.broadcasted_iota(jnp.int32, (rc, tq), 1) < row

        def rows(ref, g):
            return ref[pl.ds(c0 * SB_HEADS + h0 + g, tq, stride=SB_HEADS), :].astype(BF16)

        def stages(g):
            def scores():
                z_ref[g] = _dot_nt(q_ref[:, g * D:(g + 1) * D], rows(k_ref, g)) * scale

            def split(r0):
                sp = _softplus_fast(z_ref[g, r0:r0 + rc, :])
                if masked:
                    sp = jnp.where(valid(r0), sp, 0.0)
                hi, lo = _split2(sp)
                hi_ref[g, r0:r0 + rc, :] = hi
                lo_ref[g, r0:r0 + rc, :] = lo

            def suffix(s):
                sl = slice(s * SB_SUB, (s + 1) * SB_SUB)
                cs = _dot(hi_ref[g, :, sl], tri) + _dot(lo_ref[g, :, sl], tri)
                carry = carry_ref[g]
                cs_ref[g, :, sl] = cs + carry
                carry_ref[g] = carry + cs[:, 0:1]

            def weights(r0):
                w = jnp.exp(z_ref[g, r0:r0 + rc, :] - cs_ref[g, r0:r0 + rc, :])
                if masked:
                    w = jnp.where(valid(r0), w, 0.0)
                w_ref[g, r0:r0 + rc, :] = w.astype(BF16)

            def values():
                acc_ref[g] += _dot(w_ref[g], rows(v_ref, g))

            part = functools.partial
            return [[scores],
                    [part(split, r0) for r0 in range(0, tq, rc)],
                    [part(suffix, s) for s in reversed(range(tq // SB_SUB))],
                    [part(weights, r0) for r0 in range(0, tq, rc)],
                    [values]]

        def emit(*lists):
            n = max(len(lst) for lst in lists)
            for t in range(n):
                for lst in lists:
                    for f in lst[t * len(lst) // n:(t + 1) * len(lst) // n]:
                        f()

        per_head = [stages(g) for g in range(hg)]
        for t in range(5 + hg - 1):
            emit(*[per_head[g][t - g] for g in range(hg) if 0 <= t - g < 5])

    chunk(qi * tq, True)

    def body(it, _):
        chunk((qi - 1 - it) * tq, False)
        return 0

    lax.fori_loop(0, qi, body, 0)
    for g in range(hg):
        o_ref[:, g * D:(g + 1) * D] = acc_ref[g].astype(o_ref.dtype)


def sb_attention_prompt(q, k, v, B, L, tq=512, hg=4):
    assert L % tq == 0 and tq % SB_SUB == 0 and SB_HEADS % hg == 0 and tq % CHUNK_ROWS == 0
    nq = L // tq
    kv = lambda: pl.BlockSpec((L * SB_HEADS, SB_HEAD_DIM), lambda b, h, i: (b, 0),
                              pipeline_mode=pl.Buffered(1))
    return pl.pallas_call(
        functools.partial(_sb_prompt_kernel, tq=tq, hg=hg),
        grid=(B, SB_HEADS // hg, nq),
        in_specs=[pl.BlockSpec((tq, hg * SB_HEAD_DIM), lambda b, h, i: (b * nq + i, h)), kv(), kv()],
        out_specs=pl.BlockSpec((tq, hg * SB_HEAD_DIM), lambda b, h, i: (b * nq + i, h)),
        out_shape=jax.ShapeDtypeStruct((B * L, SB_HEADS * SB_HEAD_DIM), BF16),
        scratch_shapes=[pltpu.VMEM((hg, tq, SB_HEAD_DIM), F32), pltpu.VMEM((hg, tq, 1), F32),
                        pltpu.VMEM((hg, tq, tq), F32), pltpu.VMEM((hg, tq, tq), F32),
                        pltpu.VMEM((hg, tq, tq), BF16), pltpu.VMEM((hg, tq, tq), BF16),
                        pltpu.VMEM((hg, tq, tq), BF16)],
        compiler_params=_params("parallel", "parallel", "arbitrary"),
        name="sb_attention_prompt",
    )(q, k, v)


def _sb_decode_kernel(qbd_ref, kn_ref, vn_ref, pk_ref, pv_ref, o_ref, knew_ref, vnew_ref, acc_ref,
                      *, ls, npast):
    H, D = SB_HEADS, SB_HEAD_DIM
    W = H * ls
    qbd = qbd_ref[...]
    scale = D ** -0.5
    acc_ref[...] = jnp.zeros_like(acc_ref)

    def block(k_row, v_row, nb, masked, carry):
        zt = _dot(k_row, qbd) * scale
        sp = _softplus_fast(zt)
        if masked:
            j = lax.broadcasted_iota(jnp.int32, (nb, W), 0)
            t = jnp.bitwise_and(lax.broadcasted_iota(jnp.int32, (nb, W), 1), ls - 1)
            valid = j < t
            sp = jnp.where(valid, sp, 0.0)
        rs = lax.broadcasted_iota(jnp.int32, (nb, nb), 0)
        rj = lax.broadcasted_iota(jnp.int32, (nb, nb), 1)
        tri_t = jnp.where(rj >= rs, 1.0, 0.0).astype(BF16)
        hi, lo = _split2(sp)
        cs = _dot(tri_t, hi) + _dot(tri_t, lo)
        w = jnp.exp(zt - cs - carry)
        if masked:
            w = jnp.where(valid, w, 0.0)
        acc_ref[...] += _dot(w.T.astype(BF16), v_row)
        return carry + cs[0:1, :]

    def rows(ref, k0, n):
        return jnp.concatenate(
            [ref[pl.ds(k0 * H + h, n, stride=H), :] for h in range(H)], axis=1).astype(BF16)

    knew_ref[...] = jnp.zeros_like(knew_ref)
    vnew_ref[...] = jnp.zeros_like(vnew_ref)
    knew_ref[0:ls, :] = rows(kn_ref, 0, ls)
    vnew_ref[0:ls, :] = rows(vn_ref, 0, ls)
    carry = block(knew_ref[...], vnew_ref[...], LANES, True, jnp.zeros((1, W), F32))

    def body(it, carry):
        k0 = pl.multiple_of((npast - 1 - it) * SB_SUB, SB_SUB)
        return block(rows(pk_ref, k0, SB_SUB), rows(pv_ref, k0, SB_SUB), SB_SUB, False, carry)

    lax.fori_loop(0, npast, body, carry)
    for h in range(H):
        o_ref[:, h * D:(h + 1) * D] = acc_ref[h * ls:(h + 1) * ls, h * D:(h + 1) * D].astype(o_ref.dtype)


def sb_attention_decode(q, kn, vn, cache_k, cache_v, layer, B, ls):
    H, D = SB_HEADS, SB_HEAD_DIM
    depth, _, P = cache_k.shape[:3]
    W = H * ls
    assert P % SB_SUB == 0 and ls <= LANES and ls & (ls - 1) == 0 and ls % SUBLANES == 0
    qbd = jnp.einsum("bthd,hg->bhdgt", q.reshape(B, ls, H, D), jnp.eye(H, dtype=q.dtype))
    qbd = qbd.reshape(B, H * D, W)
    row = pl.BlockSpec((ls, H * D), lambda b: (b, 0))
    new = pl.BlockSpec((ls * H, D), lambda b: (b, 0))
    past = pl.BlockSpec((None, None, P * H, D), lambda b: (layer, b, 0, 0))
    return pl.pallas_call(
        functools.partial(_sb_decode_kernel, ls=ls, npast=P // SB_SUB),
        grid=(B,),
        in_specs=[pl.BlockSpec((None, H * D, W), lambda b: (b, 0, 0)), new, new, past, past],
        out_specs=row,
        out_shape=jax.ShapeDtypeStruct((B * ls, H * D), BF16),
        scratch_shapes=[pltpu.VMEM((LANES, H * D), BF16), pltpu.VMEM((LANES, H * D), BF16),
                        pltpu.VMEM((W, H * D), F32)],
        compiler_params=_params("parallel"),
        name="sb_attention_decode",
    )(qbd, kn, vn, cache_k.reshape(depth, B, P * H, D), cache_v.reshape(depth, B, P * H, D))


def _conv_taps(xp_ref, x3, halo, K, sl):
    xp_ref[:, SUBLANES - (K - 1):SUBLANES, :] = halo
    xp_ref[:, SUBLANES:SUBLANES + sl, :] = x3
    base = SUBLANES - (K - 1)
    return [xp_ref[:, base + k:base + k + sl, :] for k in range(K)]


def conv_halos(x, col0, state, B, L, tile):
    km1, C = state.shape[1:]
    nt = L // tile
    if nt == 1:
        return state[:, None]
    tails = x.reshape(B, nt, tile, x.shape[1])[:, :-1, tile - km1:, col0:col0 + C]
    return jnp.concatenate([state[:, None], tails], axis=1)


def _lru_kernel(lx_ref, lg_ref, halo_ref, h0_ref, cw_ref, cb_ref, wa_ref, ba_ref, wx_ref, bx_ref,
                lam_ref, o_ref, hl_ref, xp_ref, a_ref, u_ref, h_ref, hc_ref, *, tl, nl):
    i = pl.program_id(1)
    W = lx_ref.shape[1]

    @pl.when(i == 0)
    def _():
        hc_ref[...] = h0_ref[...]

    taps = _conv_taps(xp_ref, lx_ref[...][None], halo_ref[...][None], 4, tl)
    xc = cb_ref[...]
    for k in range(4):
        xc = xc + cw_ref[k:k + 1, :] * taps[k][0]

    bd = W // LRU_BLOCKS
    for n in range(LRU_BLOCKS):
        sl = slice(n * bd, (n + 1) * bd)
        xn = xc[:, sl]
        xb = xn.astype(BF16)
        r = jax.nn.sigmoid(_dot(xb, wa_ref[n].astype(BF16)) + ba_ref[:, sl])
        ig = jax.nn.sigmoid(_dot(xb, wx_ref[n].astype(BF16)) + bx_ref[:, sl])
        log_a = LRU_C * r * (-_softplus(-lam_ref[:, sl]))
        a = jnp.exp(log_a)
        u = jnp.sqrt(-jnp.tanh(log_a) * (a * a + 1.0)) * (ig * xn)
        a_ref[:, sl] = a
        u_ref[:, sl] = u

    def step(t, h):
        h = a_ref[pl.ds(t, 1), :] * h + u_ref[pl.ds(t, 1), :]
        h_ref[pl.ds(t, 1), :] = h
        return h

    h = lax.fori_loop(0, tl, step, hc_ref[...], unroll=8)
    hc_ref[...] = h
    o_ref[...] = (h_ref[...] * _gelu_tanh(lg_ref[...])).astype(o_ref.dtype)

    @pl.when(i == nl - 1)
    def _():
        hl_ref[...] = h


def lru_branch(src, lx_col, lg_col, state, h0, cw, cb, wa, ba, wx, bx, lam, B, L, tl=256):
    W = state.shape[2]
    tl = min(tl, L)
    nl = L // tl
    halo = conv_halos(src, lx_col * W, state, B, L, tl)
    kern = functools.partial(_lru_kernel, tl=tl, nl=nl)
    vec = lambda: pl.BlockSpec((1, W), lambda b, i: (0, 0))
    blk = lambda: pl.BlockSpec(wa.shape, lambda b, i: (0, 0, 0))
    lx, lg = src, src
    o, hl = pl.pallas_call(
        kern,
        grid=(B, nl),
        in_specs=[
            pl.BlockSpec((tl, W), lambda b, i: (b * nl + i, lx_col)),
            pl.BlockSpec((tl, W), lambda b, i: (b * nl + i, lg_col)),
            pl.BlockSpec((None, None, 3, W), lambda b, i: (b, i, 0, 0)),
            pl.BlockSpec((None, 1, W), lambda b, i: (b, 0, 0)),
            pl.BlockSpec((4, W), lambda b, i: (0, 0)),
            vec(), blk(), vec(), blk(), vec(), vec(),
        ],
        out_specs=[
            pl.BlockSpec((tl, W), lambda b, i: (b * nl + i, 0)),
            pl.BlockSpec((None, 1, W), lambda b, i: (b, 0, 0)),
        ],
        out_shape=[
            jax.ShapeDtypeStruct((B * L, W), BF16),
            jax.ShapeDtypeStruct((B, 1, W), F32),
        ],
        scratch_shapes=[
            pltpu.VMEM((1, tl + SUBLANES, W), F32),
            pltpu.VMEM((tl, W), F32),
            pltpu.VMEM((tl, W), F32),
            pltpu.VMEM((tl, W), F32),
            pltpu.VMEM((1, W), F32),
        ],
        compiler_params=_params("parallel", "arbitrary"),
        name="lru_branch",
    )(lx, lg, halo, h0.reshape(B, 1, W), cw, cb.reshape(1, W), wa, ba.reshape(1, W),
      wx, bx.reshape(1, W), lam.reshape(1, W))
    return o, hl.reshape(B, W)


def _ssd_kernel(z_ref, xbc_ref, dt_ref, halo_ref, h0_ref, cw_ref, cb_ref, dtb_ref, alog_ref,
                dexp_ref, nw_ref, o_ref, hl_ref, xp_ref, y_ref, st_ref, *, q, nc, n_valid):
    c = pl.program_id(1)
    inner = SSD_HEADS * SSD_HEAD_DIM
    gw = inner // SSD_GROUPS
    epg = SSD_HEADS // SSD_GROUPS
    n = SSD_STATE

    @pl.when(c == 0)
    def _():
        st_ref[...] = h0_ref[...]

    taps = _conv_taps(xp_ref, xbc_ref[...][None], halo_ref[...][None], 4, q)
    xc = cb_ref[...]
    for k in range(4):
        xc = xc + cw_ref[k:k + 1, :] * taps[k][0]
    xc = _silu(xc)
    xs = xc[:, :inner]
    bm = xc[:, inner:inner + SSD_GROUPS * n]
    cm = xc[:, inner + SSD_GROUPS * n:]

    rowi = lax.broadcasted_iota(jnp.int32, (q, LANES), 0)
    lanei = lax.broadcasted_iota(jnp.int32, (q, LANES), 1)
    dts = jnp.where(lanei < SSD_HEADS, _softplus(dt_ref[...] + dtb_ref[...]), 0.0)
    if n_valid < q:
        dts = jnp.where(rowi < n_valid, dts, 0.0)
    dta = dts * (-jnp.exp(alog_ref[...]))

    ti = lax.broadcasted_iota(jnp.int32, (q, q), 0)
    si = lax.broadcasted_iota(jnp.int32, (q, q), 1)
    causal = si <= ti
    lower = jnp.where(causal, 1.0, 0.0).astype(BF16)
    cum = sum(_dot(lower, p) for p in _split3(dta))
    cum_t = cum.T
    cum_last = cum[q - 1:q, :]
    exp_cum = jnp.exp(cum)
    decay_end = jnp.exp(cum_last - cum)

    eh = lax.broadcasted_iota(jnp.int32, (LANES, inner), 0)
    ec = lax.broadcasted_iota(jnp.int32, (LANES, inner), 1)
    expand = jnp.where(ec // SSD_HEAD_DIM == eh, 1.0, 0.0).astype(BF16)

    def expand_heads(a):
        return sum(_dot(p, expand) for p in _split3(a))

    dts_e = expand_heads(dts)
    exp_cum_e = expand_heads(exp_cum)
    decay_end_e = expand_heads(decay_end)
    chunk_decay_e = exp_cum_e[q - 1:q, :]

    xdt = xs * dts_e
    xdt_b = xdt.astype(BF16)
    xend_b = (decay_end_e * xdt).astype(BF16)

    for g in range(SSD_GROUPS):
        bg = bm[:, g * n:(g + 1) * n]
        cg_b = cm[:, g * n:(g + 1) * n].astype(BF16)
        gs = slice(g * gw, (g + 1) * gw)
        cb = _dot_nt(cg_b, bg.astype(BF16))
        st = st_ref[g]
        y_inter = _dot(cg_b, st.astype(BF16)) * exp_cum_e[:, gs]
        for e in range(epg):
            h = g * epg + e
            hs = slice(h * SSD_HEAD_DIM, (h + 1) * SSD_HEAD_DIM)
            seg = cum[:, h:h + 1] - cum_t[h:h + 1, :]
            decay = jnp.where(causal, jnp.exp(seg), 0.0)
            wts = (cb * decay).astype(BF16)
            y_ref[:, hs] = _dot(wts, xdt_b[:, hs]) + y_inter[:, e * SSD_HEAD_DIM:(e + 1) * SSD_HEAD_DIM]
        bg_t = bg.T.astype(BF16)
        st_ref[g] = chunk_decay_e[:, gs] * st + _dot(bg_t, xend_b[:, gs])

    y = y_ref[...] + dexp_ref[...] * xs
    yg = y * _silu(z_ref[...])
    for g in range(SSD_GROUPS):
        gs = slice(g * gw, (g + 1) * gw)
        part = yg[:, gs]
        ms = jnp.mean(part * part, axis=-1, keepdims=True)
        o_ref[:, gs] = (part * lax.rsqrt(ms + NORM_EPS) * nw_ref[:, gs]).astype(o_ref.dtype)

    @pl.when(c == nc - 1)
    def _():
        hl_ref[...] = st_ref[...]


def _pad_rows(x, B, L, Lp):
    if Lp == L:
        return x
    C = x.shape[1]
    return jnp.pad(x.reshape(B, L, C), ((0, 0), (0, Lp - L), (0, 0))).reshape(B * Lp, C)


def ssd_branch(src, z_col, xbc_col, dt_col, state, h0, cw, cb, dt_bias, a_log, d, norm_w, B, L, q=128):
    inner = norm_w.shape[0]
    cd = state.shape[2]
    nh = SSD_HEADS
    Lp = max(L, q) if L % q else L
    n_valid = min(L, q)
    nc = Lp // q
    gw = inner // SSD_GROUPS
    epg = SSD_HEADS // SSD_GROUPS
    halo = conv_halos(src, xbc_col, state, B, L, q) if L >= q else state[:, None]
    if Lp != L:
        zp = _pad_rows(src[:, z_col:z_col + inner], B, L, Lp)
        xbcp = _pad_rows(src[:, xbc_col:xbc_col + cd], B, L, Lp)
        dtp = _pad_rows(src[:, dt_col:dt_col + LANES], B, L, Lp)
        z_col = xbc_col = dt_col = 0
    else:
        zp = xbcp = dtp = src
    z_blk, xbc_blk, dt_block = z_col // inner, xbc_col // cd, dt_col // LANES
    h0t = h0.reshape(B, SSD_GROUPS, epg, SSD_HEAD_DIM, SSD_STATE).transpose(0, 1, 4, 2, 3)
    h0t = h0t.reshape(B, SSD_GROUPS, SSD_STATE, gw)
    pad1 = lambda a: jnp.pad(a, (0, LANES - nh)).reshape(1, LANES)
    kern = functools.partial(_ssd_kernel, q=q, nc=nc, n_valid=n_valid)
    row = lambda w, blk=0: pl.BlockSpec((q, w), lambda b, c: (b * nc + c, blk))
    vec = lambda w: pl.BlockSpec((1, w), lambda b, c: (0, 0))
    o, hl = pl.pallas_call(
        kern,
        grid=(B, nc),
        in_specs=[
            row(inner, z_blk), row(cd, xbc_blk), row(LANES, dt_block),
            pl.BlockSpec((None, None, 3, cd), lambda b, c: (b, c, 0, 0)),
            pl.BlockSpec((None, SSD_GROUPS, SSD_STATE, gw), lambda b, c: (b, 0, 0, 0)),
            pl.BlockSpec((4, cd), lambda b, c: (0, 0)),
            vec(cd), vec(LANES), vec(LANES), vec(inner), vec(inner),
        ],
        out_specs=[
            row(inner),
            pl.BlockSpec((None, SSD_GROUPS, SSD_STATE, gw), lambda b, c: (b, 0, 0, 0)),
        ],
        out_shape=[
            jax.ShapeDtypeStruct((B * Lp, inner), BF16),
            jax.ShapeDtypeStruct((B, SSD_GROUPS, SSD_STATE, gw), F32),
        ],
        scratch_shapes=[
            pltpu.VMEM((1, q + SUBLANES, cd), F32),
            pltpu.VMEM((q, inner), F32),
            pltpu.VMEM((SSD_GROUPS, SSD_STATE, gw), F32),
        ],
        compiler_params=_params("parallel", "arbitrary"),
        name="ssd_branch",
    )(zp, xbcp, dtp, halo, h0t, cw, cb.reshape(1, cd), pad1(dt_bias), pad1(a_log),
      jnp.repeat(d, SSD_HEAD_DIM).reshape(1, inner), norm_w.reshape(1, inner))
    if Lp != L:
        o = o.reshape(B, Lp, inner)[:, :L].reshape(B * L, inner)
    hl = hl.reshape(B, SSD_GROUPS, SSD_STATE, epg, SSD_HEAD_DIM).transpose(0, 1, 3, 4, 2)
    return o, hl.reshape(B, SSD_HEADS, SSD_HEAD_DIM, SSD_STATE)


def _merge_kernel(x_ref, g_ref, a_ref, b_ref, c_ref, wg0_ref, wg1_ref, wg2_ref, wb_ref, o_ref, xn_ref):
    @pl.when(pl.program_id(1) == 0)
    def _():
        x = x_ref[...]
        ms = jnp.mean(x * x, axis=-1, keepdims=True)
        xn_ref[...] = (x * lax.rsqrt(ms + NORM_EPS) * g_ref[...]).astype(BF16)

    xn = xn_ref[...]
    acc = None
    for n, (br, wg) in enumerate(((a_ref, wg0_ref), (b_ref, wg1_ref), (c_ref, wg2_ref))):
        gate = _dot_nt(xn, wg[...].astype(BF16))
        t = _dot(br[...], wb_ref[n].astype(BF16)) * jax.nn.sigmoid(gate)
        acc = t if acc is None else acc + t
    o_ref[...] = acc.astype(o_ref.dtype)


def gated_merge(x, g, oa, ob, oc, w_gate_t, wb, layer, tm=1024, tn=256):
    T, K = oa.shape
    D = x.shape[1]
    N = wb.shape[-1]
    tm = min(tm, T)
    nj = N // tn
    br = lambda: pl.BlockSpec((tm, K), lambda i, j: (i, 0))
    gate = lambda n: pl.BlockSpec((None, tn, D), lambda i, j: (layer, n * nj + j, 0))
    return pl.pallas_call(
        _merge_kernel,
        grid=(T // tm, nj),
        in_specs=[pl.BlockSpec((tm, D), lambda i, j: (i, 0), pipeline_mode=pl.Buffered(1)),
                  pl.BlockSpec((1, D), lambda i, j: (0, 0)),
                  br(), br(), br(), gate(0), gate(1), gate(2),
                  pl.BlockSpec((None, 3, K, tn), lambda i, j: (layer, 0, 0, j))],
        out_specs=pl.BlockSpec((tm, tn), lambda i, j: (i, j)),
        out_shape=jax.ShapeDtypeStruct((T, N), BF16),
        scratch_shapes=[pltpu.VMEM((tm, D), BF16)],
        compiler_params=_params("parallel", "arbitrary"),
        name="gated_merge",
    )(x, g.reshape(1, D), oa, ob, oc, w_gate_t, w_gate_t, w_gate_t, wb)


def _xattn_kernel(q_ref, mk_ref, mv_ref, o_ref):
    hd = q_ref.shape[1] // XA_HEADS
    scale = hd ** -0.5
    for h in range(XA_HEADS):
        sl = slice(h * hd, (h + 1) * hd)
        kh, vh = (mk_ref[:, sl], mv_ref[:, sl]) if len(mk_ref.shape) == 2 else (mk_ref[:, h, :], mv_ref[:, h, :])
        s = _dot_nt(q_ref[:, sl].astype(BF16), kh.astype(BF16)) * scale
        m = jnp.max(s, axis=-1, keepdims=True)
        p = jnp.exp(s - m)
        p = p / jnp.sum(p, axis=-1, keepdims=True)
        o_ref[:, sl] = _dot(p.astype(BF16), vh.astype(BF16)).astype(o_ref.dtype)


def cross_attention(q, mk, mv, layer, B, L, tq=512):
    D = q.shape[1]
    tq = min(tq, L)
    nq = L // tq
    if layer is None:
        mem = lambda: pl.BlockSpec((None,) + mk.shape[1:], lambda b, i: (b, 0, 0))
    else:
        mem = lambda: pl.BlockSpec((None, None) + mk.shape[2:], lambda b, i: (layer, b, 0, 0, 0))
    return pl.pallas_call(
        _xattn_kernel,
        grid=(B, nq),
        in_specs=[pl.BlockSpec((tq, D), lambda b, i: (b * nq + i, 0)), mem(), mem()],
        out_specs=pl.BlockSpec((tq, D), lambda b, i: (b * nq + i, 0)),
        out_shape=jax.ShapeDtypeStruct((B * L, D), BF16),
        compiler_params=_params("parallel", "arbitrary"),
        name="cross_attention",
    )(q, mk, mv)


def _mem_kv_kernel(x_ref, g_ref, wk_ref, wv_ref, k_ref, v_ref, k2_ref, v2_ref, xn_ref):
    j = pl.program_id(1)

    @pl.when(j == 0)
    def _():
        x = x_ref[...]
        ms = jnp.mean(x * x, axis=-1, keepdims=True)
        xn_ref[...] = (x * lax.rsqrt(ms + NORM_EPS) * g_ref[...]).astype(BF16)

    xn = xn_ref[...]
    k = _dot(xn, wk_ref[...].astype(BF16))
    v = _dot(xn, wv_ref[...].astype(BF16))
    k2_ref[...] = k.astype(k2_ref.dtype)
    v2_ref[...] = v.astype(v2_ref.dtype)
    for h in range(XA_HEADS):
        @pl.when(j == h)
        def _(h=h):
            k_ref[:, h, :] = k
            v_ref[:, h, :] = v


def mem_kv(mem, g, wk, wv, layer, B, M):
    D = mem.shape[1]
    hd = D // XA_HEADS
    w = lambda: pl.BlockSpec((None, D, hd), lambda b, j: (layer, 0, j))
    o4 = lambda: pl.BlockSpec((None, M, XA_HEADS, hd), lambda b, j: (b, 0, 0, 0))
    o2 = lambda: pl.BlockSpec((None, M, hd), lambda b, j: (b, 0, j))
    return pl.pallas_call(
        _mem_kv_kernel,
        grid=(B, XA_HEADS),
        in_specs=[pl.BlockSpec((M, D), lambda b, j: (b, 0)), pl.BlockSpec((1, D), lambda b, j: (0, 0)),
                  w(), w()],
        out_specs=[o4(), o4(), o2(), o2()],
        out_shape=[jax.ShapeDtypeStruct((B, M, XA_HEADS, hd), F32)] * 2
        + [jax.ShapeDtypeStruct((B, M, D), BF16)] * 2,
        scratch_shapes=[pltpu.VMEM((M, D), BF16)],
        compiler_params=_params("parallel", "arbitrary"),
        name="mem_kv",
    )(mem, g.reshape(1, D), wk, wv)


def _ffn_up_kernel(x_ref, g_ref, wg_ref, wv_ref, sg_ref, sv_ref, cwg_ref, cwv_ref, cbg_ref, cbv_ref,
                   act_ref, tg_ref, tv_ref, xn_ref, xp_ref, carry_ref, *, nseg, sl, tpb):
    i = pl.program_id(0)
    f = pl.program_id(1)
    tm, tf = act_ref.shape

    @pl.when(f == 0)
    def _():
        x = x_ref[...]
        ms = jnp.mean(x * x, axis=-1, keepdims=True)
        xn_ref[...] = (x * lax.rsqrt(ms + NORM_EPS) * g_ref[...]).astype(BF16)

    if tpb > 1:
        @pl.when((i == 0) & (f == 0))
        def _():
            carry_ref[...] = jnp.zeros_like(carry_ref)

    xn = xn_ref[...]
    base = SUBLANES - 2
    starts = i % tpb == 0
    ts = xp_ref.shape[-1]

    def half(s, c, w_ref, st_ref, cw_ref, cb_ref, tail_ref):
        cols = slice(s * ts, (s + 1) * ts)
        xp = xp_ref.at[2 * s + c]
        u3 = _dot(xn, w_ref[:, cols].astype(BF16)).reshape(nseg, sl, ts)
        halo = st_ref[:, :, cols]
        if tpb > 1:
            halo = jnp.where(starts, halo, carry_ref[c, f, :, :, cols])
        xp[:, base:SUBLANES, :] = halo
        xp[:, SUBLANES:SUBLANES + sl, :] = u3
        y = cb_ref[:, cols]
        for k in range(3):
            y = y + cw_ref[k:k + 1, cols] * xp[:, base + k:base + k + sl, :].reshape(tm, ts)
        tail = xp[:, SUBLANES + sl - 2:SUBLANES + sl, :]
        if tpb > 1:
            carry_ref[c, f, :, :, cols] = tail
        tail_ref[:, :, cols] = tail
        return y

    for s in range(tf // ts):
        gate = half(s, 0, wg_ref, sg_ref, cwg_ref, cbg_ref, tg_ref)
        val = half(s, 1, wv_ref, sv_ref, cwv_ref, cbv_ref, tv_ref)
        act_ref[:, s * ts:(s + 1) * ts] = (_silu(gate) * val).astype(act_ref.dtype)


def ffn_up(x, g, w_up, layer, state, cw, cb, B, L, tm=1024, tf=512):
    T, D = x.shape
    F2 = w_up.shape[-1]
    F = F2 // 2
    if L >= tm:
        nseg, sl, tpb = 1, tm, L // tm
    else:
        nseg, sl, tpb = min(tm // L, B), L, 1
        tm = nseg * sl
    nt, nf = T // tm, F // tf
    ts = min(tf, MXU_WIDTH)
    st = state.reshape(B // nseg, nseg, 2, F2)
    cb2 = cb.reshape(1, F2)
    kern = functools.partial(_ffn_up_kernel, nseg=nseg, sl=sl, tpb=tpb)
    wspec = lambda off: pl.BlockSpec((None, D, tf), lambda i, f: (layer, 0, off + f))
    sspec = lambda off: pl.BlockSpec((None, nseg, 2, tf), lambda i, f: (i // tpb, 0, 0, off + f))
    cwspec = lambda off: pl.BlockSpec((3, tf), lambda i, f: (0, off + f))
    cbspec = lambda off: pl.BlockSpec((1, tf), lambda i, f: (0, off + f))
    tspec = lambda: pl.BlockSpec((None, nseg, 2, tf), lambda i, f: (i, 0, 0, f))
    act, tg, tv = pl.pallas_call(
        kern,
        grid=(nt, nf),
        in_specs=[
            pl.BlockSpec((tm, D), lambda i, f: (i, 0), pipeline_mode=pl.Buffered(1)),
            pl.BlockSpec((1, D), lambda i, f: (0, 0)),
            wspec(0), wspec(nf), sspec(0), sspec(nf), cwspec(0), cwspec(nf), cbspec(0), cbspec(nf),
        ],
        out_specs=[pl.BlockSpec((tm, tf), lambda i, f: (i, f)), tspec(), tspec()],
        out_shape=[
            jax.ShapeDtypeStruct((T, F), BF16),
            jax.ShapeDtypeStruct((nt, nseg, 2, F), F32),
            jax.ShapeDtypeStruct((nt, nseg, 2, F), F32),
        ],
        scratch_shapes=[
            pltpu.VMEM((tm, D), BF16),
            pltpu.VMEM((2 * (tf // ts), nseg, sl + SUBLANES, ts), F32),
            pltpu.VMEM((2, nf, nseg, 2, tf), F32),
        ],
        compiler_params=_params("arbitrary", "arbitrary"),
        name="ffn_up",
    )(x, g.reshape(1, D), w_up, w_up, st, st, cw, cw, cb2, cb2)
    last = lambda t: t.reshape(B // nseg, tpb, nseg, 2, F)[:, -1].reshape(B, 2, F)
    return act, jnp.concatenate([last(tg), last(tv)], axis=-1)


def _last_rows(x, col0, state, B, L):
    km1, C = state.shape[1:]
    x3 = x.reshape(B, L, x.shape[1])
    if L >= km1:
        return x3[:, L - km1:, col0:col0 + C]
    return jnp.concatenate([state, x3[:, :, col0:col0 + C]], axis=1)[:, -km1:]


def trunk_layer(x, B, L, mk, mv, mem_layer, past_k, past_v, lru_buf, lru_h, ssd_buf, ssd_h, ffn_buf, w, l):
    mix = SB_HEADS * SB_HEAD_DIM
    inner = SSD_HEADS * SSD_HEAD_DIM
    cd = inner + 2 * SSD_GROUPS * SSD_STATE
    D = x.shape[1]
    tn = 512
    q, k, v, rest = in_proj(x, w["norm_mix"][l], w["w_in_t"], l, (mix, mix, mix, 2 * mix + inner + cd + tn),
                            (BF16, F32, F32, F32), (False, True, True, False), SB_HEAD_DIM, tn=tn)
    c_z, c_xbc, c_dt = 2 * mix, 2 * mix + inner, 2 * mix + inner + cd

    if past_k is None:
        o_a = sb_attention_prompt(q, k, v, B, L)
    else:
        o_a = sb_attention_decode(q, k, v, past_k, past_v, l, B, L)
    o_b, lru_h_new = lru_branch(rest, 0, 1, lru_buf, lru_h, w["lru_conv_w"][l], w["lru_conv_b"][l],
                                w["lru_wa"][l], w["lru_ba"][l], w["lru_wx"][l], w["lru_bx"][l],
                                w["lru_lam"][l], B, L)
    o_c, ssd_h_new = ssd_branch(rest, c_z, c_xbc, c_dt, ssd_buf, ssd_h, w["ssd_conv_w"][l],
                                w["ssd_conv_b"][l], w["ssd_dt_bias"][l], w["ssd_a_log"][l], w["ssd_d"][l],
                                w["ssd_norm"][l], B, L)
    merged = gated_merge(x, w["norm_mix"][l], o_a, o_b, o_c, w["w_gate_t"], w["w_branch"], l)
    h = matmul_res(merged, w["w_out"], l, x, tm=2048)
    qx = norm_matmul(h, w["norm_xattn"][l], w["w_xq"], l, out_dtype=BF16, tn=1024)
    o_x = cross_attention(qx, mk, mv, mem_layer, B, L)
    h = matmul_res(o_x, w["w_xo"], l, h, tm=2048)
    act, ffn_new = ffn_up(h, w["norm_ffn"][l], w["w_up"], l, ffn_buf, w["ffn_conv_w"][l],
                          w["ffn_conv_b"][l], B, L)
    out = matmul_res(act, w["w_down"], l, h, tn=256)
    states = (k.reshape(B, L, SB_HEADS, SB_HEAD_DIM), v.reshape(B, L, SB_HEADS, SB_HEAD_DIM),
              _last_rows(rest, 0, lru_buf, B, L), lru_h_new, _last_rows(rest, c_xbc, ssd_buf, B, L),
              ssd_h_new, ffn_new)
    return out, states


def kernel(x_prompt, x_sample, mem_prompt, cache_sb_k, cache_sb_v, cache_mem_k, cache_mem_v, state_lru_conv, state_lru_h, state_ssd_conv, state_ssd, state_ffn_conv, norm_mix, w_in, lru_conv_w, lru_conv_b, lru_wa, lru_ba, lru_wx, lru_bx, lru_lam, ssd_conv_w, ssd_conv_b, ssd_dt_bias, ssd_a_log, ssd_d, ssd_norm, w_branch, w_out, norm_xattn, norm_mem, w_xq, w_xk, w_xv, w_xo, norm_ffn, w_up, ffn_conv_w, ffn_conv_b, w_down, norm_final):
    depth = w_in.shape[0]
    Bp, Lp, D = x_prompt.shape
    Bs, Ls, _ = x_sample.shape
    M = mem_prompt.shape[1]
    mix = SB_HEADS * SB_HEAD_DIM
    inner = SSD_HEADS * SSD_HEAD_DIM
    cd = inner + 2 * SSD_GROUPS * SSD_STATE
    f2 = w_up.shape[2]
    xa_hd = D // XA_HEADS

    yp = x_prompt.reshape(Bp * Lp, D)
    ys = x_sample.reshape(Bs * Ls, D)
    mem = mem_prompt.reshape(Bp * M, D)
    st_p, st_s, mk_list, mv_list = [], [], [], []
    w_in_t = jnp.swapaxes(w_in, 1, 2)
    w_gate_t = w_in_t[:, 5 * mix + inner + cd + SSD_HEADS:]
    w = dict(w_in_t=w_in_t, w_gate_t=w_gate_t, norm_mix=norm_mix, lru_conv_w=lru_conv_w, lru_conv_b=lru_conv_b,
             lru_wa=lru_wa, lru_ba=lru_ba, lru_wx=lru_wx, lru_bx=lru_bx, lru_lam=lru_lam,
             ssd_conv_w=ssd_conv_w, ssd_conv_b=ssd_conv_b, ssd_dt_bias=ssd_dt_bias, ssd_a_log=ssd_a_log,
             ssd_d=ssd_d, ssd_norm=ssd_norm, w_branch=w_branch, w_out=w_out, norm_xattn=norm_xattn,
             w_xq=w_xq, w_xo=w_xo, norm_ffn=norm_ffn, w_up=w_up, ffn_conv_w=ffn_conv_w,
             ffn_conv_b=ffn_conv_b, w_down=w_down)
    for l in range(depth):
        mk_p, mv_p, mk2, mv2 = mem_kv(mem, norm_mem[l], w_xk, w_xv, l, Bp, M)
        zeros = lambda *s: jnp.zeros(s, F32)
        yp, sp = trunk_layer(yp, Bp, Lp, mk2, mv2, None, None, None, zeros(Bp, 3, mix),
                             zeros(Bp, mix), zeros(Bp, 3, cd), zeros(Bp, SSD_HEADS, SSD_HEAD_DIM, SSD_STATE),
                             zeros(Bp, 2, f2), w, l)
        ys, ss = trunk_layer(ys, Bs, Ls, cache_mem_k, cache_mem_v, l,
                             cache_sb_k, cache_sb_v, state_lru_conv[l], state_lru_h[l],
                             state_ssd_conv[l], state_ssd[l], state_ffn_conv[l], w, l)
        st_p.append(sp)
        st_s.append(ss)
        mk_list.append(mk_p)
        mv_list.append(mv_p)
    y_prompt = rmsnorm(yp, norm_final).reshape(Bp, Lp, D)
    y_sample = rmsnorm(ys, norm_final).reshape(Bs, Ls, D)

    def stk(lst, i):
        return jnp.stack([s[i] for s in lst], axis=0)

    return (y_prompt, y_sample,
            stk(st_p, 0), stk(st_p, 1), stk(st_p, 2), stk(st_p, 3), stk(st_p, 4), stk(st_p, 5), stk(st_p, 6),
            jnp.stack(mk_list, axis=0), jnp.stack(mv_list, axis=0),
            stk(st_s, 0), stk(st_s, 1), stk(st_s, 2), stk(st_s, 3), stk(st_s, 4), stk(st_s, 5), stk(st_s, 6))
```

```python
import functools
import math

import jax
import jax.numpy as jnp
from jax import lax
from jax.experimental import pallas as pl
from jax.experimental.pallas import tpu as pltpu

F32 = jnp.float32
BF16 = jnp.bfloat16

NORM_EPS = 1e-6
LRU_C = 8.0
LANES = 128
SUBLANES = 8
MXU_WIDTH = 256
CHUNK_ROWS = 32
VMEM_LIMIT_BYTES = 56 * 1024 * 1024

SB_HEADS = 8
SB_HEAD_DIM = 128
LRU_BLOCKS = 8
SSD_HEADS = 16
SSD_HEAD_DIM = 64
SSD_GROUPS = 2
SSD_STATE = 128
XA_HEADS = 4


def _params(*sem):
    return pltpu.CompilerParams(dimension_semantics=sem, vmem_limit_bytes=VMEM_LIMIT_BYTES)


def _dot(a, b):
    return jnp.dot(a, b, preferred_element_type=F32)


def _dot_nt(a, b):
    return lax.dot_general(a, b, (((1,), (1,)), ((), ())), preferred_element_type=F32)


def _split3(x):
    hi = x.astype(BF16)
    r = x - hi.astype(F32)
    mid = r.astype(BF16)
    lo = (r - mid.astype(F32)).astype(BF16)
    return hi, mid, lo


def _softplus(x):
    return jnp.maximum(x, 0.0) + jnp.log1p(jnp.exp(-jnp.abs(x)))


def _silu(x):
    return x * jax.nn.sigmoid(x)


def _gelu_tanh(x):
    c = math.sqrt(2.0 / math.pi)
    return 0.5 * x * (1.0 + jnp.tanh(c * (x + 0.044715 * (x * x * x))))


def _norm_matmul_kernel(x_ref, g_ref, w_ref, o_ref, xn_ref, *, transposed):
    @pl.when(pl.program_id(1) == 0)
    def _():
        x = x_ref[...]
        ms = jnp.mean(x * x, axis=-1, keepdims=True)
        xn_ref[...] = (x * lax.rsqrt(ms + NORM_EPS) * g_ref[...]).astype(BF16)

    w = w_ref[...].astype(BF16)
    acc = _dot_nt(xn_ref[...], w) if transposed else _dot(xn_ref[...], w)
    o_ref[...] = acc.astype(o_ref.dtype)


def _layer_spec(w, layer, block, index):
    if layer is None:
        return pl.BlockSpec(block, index)
    return pl.BlockSpec((None,) + block, lambda *ids: (layer,) + index(*ids))


def norm_matmul(x, g, w, layer=None, out_dtype=F32, transposed=False, tm=1024, tn=512):
    T, D = x.shape
    N = w.shape[-2] if transposed else w.shape[-1]
    tm = min(tm, T)
    tn = min(tn, N)
    w_spec = (_layer_spec(w, layer, (tn, D), lambda i, j: (j, 0)) if transposed
              else _layer_spec(w, layer, (D, tn), lambda i, j: (0, j)))
    return pl.pallas_call(
        functools.partial(_norm_matmul_kernel, transposed=transposed),
        grid=(T // tm, N // tn),
        in_specs=[
            pl.BlockSpec((tm, D), lambda i, j: (i, 0)),
            pl.BlockSpec((1, D), lambda i, j: (0, 0)),
            w_spec,
        ],
        out_specs=pl.BlockSpec((tm, tn), lambda i, j: (i, j)),
        out_shape=jax.ShapeDtypeStruct((T, N), out_dtype),
        scratch_shapes=[pltpu.VMEM((tm, D), BF16)],
        compiler_params=_params("parallel", "arbitrary"),
        name="norm_matmul",
    )(x, g.reshape(1, D), w)


def _in_proj_kernel(x_ref, g_ref, w_ref, *refs, ranges, by_head, hd):
    out_refs, xn_ref = refs[:-1], refs[-1]
    j = pl.program_id(1)
    tm, tn = xn_ref.shape[0], w_ref.shape[0]

    @pl.when(j == 0)
    def _():
        x = x_ref[...]
        ms = jnp.mean(x * x, axis=-1, keepdims=True)
        xn_ref[...] = (x * lax.rsqrt(ms + NORM_EPS) * g_ref[...]).astype(BF16)

    acc = _dot_nt(xn_ref[...], w_ref[...].astype(BF16))
    for o_ref, (lo, hi), heads in zip(out_refs, ranges, by_head):
        @pl.when((j >= lo) & (j < hi))
        def _(o_ref=o_ref, lo=lo, heads=heads):
            if heads:
                nh = (hi - lo) * (tn // hd)
                for c in range(tn // hd):
                    o_ref[pl.ds((j - lo) * (tn // hd) + c, tm, stride=nh), :] = acc[:, c * hd:(c + 1) * hd]
            else:
                o_ref[...] = acc.astype(o_ref.dtype)


def in_proj(x, g, w_t, layer, widths, dtypes, by_head, hd, tm=1024, tn=512):
    T, D = x.shape
    tm = min(tm, T)
    ranges, nblocks = [], 0
    for wd in widths:
        assert wd % tn == 0
        ranges.append((nblocks, nblocks + wd // tn))
        nblocks += wd // tn
    out_specs, out_shape = [], []
    for (lo, hi), wd, dt, heads in zip(ranges, widths, dtypes, by_head):
        if heads:
            out_specs.append(pl.BlockSpec((tm * (wd // hd), hd), lambda i, j: (i, 0),
                                          pipeline_mode=pl.Buffered(1)))
            out_shape.append(jax.ShapeDtypeStruct((T * (wd // hd), hd), dt))
        else:
            out_specs.append(pl.BlockSpec(
                (tm, tn), lambda i, j, lo=lo, n=hi - lo: (i, jnp.clip(j - lo, 0, n - 1))))
            out_shape.append(jax.ShapeDtypeStruct((T, wd), dt))
    return pl.pallas_call(
        functools.partial(_in_proj_kernel, ranges=tuple(ranges), by_head=tuple(by_head), hd=hd),
        grid=(T // tm, nblocks),
        in_specs=[
            pl.BlockSpec((tm, D), lambda i, j: (i, 0)),
            pl.BlockSpec((1, D), lambda i, j: (0, 0)),
            pl.BlockSpec((None, tn, D), lambda i, j: (layer, j, 0)),
        ],
        out_specs=out_specs,
        out_shape=out_shape,
        scratch_shapes=[pltpu.VMEM((tm, D), BF16)],
        compiler_params=_params("parallel", "arbitrary"),
        name="in_proj",
    )(x, g.reshape(1, D), w_t)


def _matmul_res_kernel(x_ref, w_ref, r_ref, o_ref):
    o_ref[...] = r_ref[...] + _dot(x_ref[...], w_ref[...].astype(BF16))


def matmul_res(x, w, layer, res, tm=1024, tn=512):
    T, K = x.shape
    N = w.shape[-1]
    tm = min(tm, T)
    tn = min(tn, N)
    return pl.pallas_call(
        _matmul_res_kernel,
        grid=(T // tm, N // tn),
        in_specs=[
            pl.BlockSpec((tm, K), lambda i, j: (i, 0)),
            _layer_spec(w, layer, (K, tn), lambda i, j: (0, j)),
            pl.BlockSpec((tm, tn), lambda i, j: (i, j)),
        ],
        out_specs=pl.BlockSpec((tm, tn), lambda i, j: (i, j)),
        out_shape=jax.ShapeDtypeStruct((T, N), F32),
        compiler_params=_params("parallel", "arbitrary"),
        name="matmul_res",
    )(x, w, res)


def _rmsnorm_kernel(x_ref, g_ref, o_ref):
    x = x_ref[...]
    ms = jnp.mean(x * x, axis=-1, keepdims=True)
    o_ref[...] = x * lax.rsqrt(ms + NORM_EPS) * g_ref[...]


def rmsnorm(x, g, tm=512):
    T, D = x.shape
    tm = min(tm, T)
    return pl.pallas_call(
        _rmsnorm_kernel,
        grid=(T // tm,),
        in_specs=[pl.BlockSpec((tm, D), lambda i: (i, 0)), pl.BlockSpec((1, D), lambda i: (0, 0))],
        out_specs=pl.BlockSpec((tm, D), lambda i: (i, 0)),
        out_shape=jax.ShapeDtypeStruct((T, D), F32),
        compiler_params=_params("parallel"),
        name="rmsnorm",
    )(x, g.reshape(1, D))


SB_SUB = MXU_WIDTH


def _softplus_fast(z):
    return jnp.maximum(z, 0.0) + jnp.log(1.0 + jnp.exp(-jnp.abs(z)))


def _split2(x):
    hi = x.astype(BF16)
    return hi, (x - hi.astype(F32)).astype(BF16)


def _sb_prompt_kernel(q_ref, k_ref, v_ref, o_ref, acc_ref, carry_ref, z_ref, cs_ref, hi_ref, lo_ref, w_ref,
                      *, tq, hg):
    h0 = pl.program_id(1) * hg
    qi = pl.program_id(2)
    D = SB_HEAD_DIM
    rj = lax.broadcasted_iota(jnp.int32, (SB_SUB, SB_SUB), 0)
    rs = lax.broadcasted_iota(jnp.int32, (SB_SUB, SB_SUB), 1)
    tri = jnp.where(rj >= rs, 1.0, 0.0).astype(BF16)
    scale = D ** -0.5
    acc_ref[...] = jnp.zeros_like(acc_ref)
    carry_ref[...] = jnp.zeros_like(carry_ref)

    def chunk(c0, masked):
        c0 = pl.multiple_of(c0, tq)
        rc = CHUNK_ROWS

        def valid(r0):
            row = r0 + lax.broadcasted_iota(jnp.int32, (rc, tq), 0)
            return lax.broadcasted_iota(jnp.int32, (rc, tq), 1) < row

        def rows(ref, g):
            return ref[pl.ds(c0 * SB_HEADS + h0 + g, tq, stride=SB_HEADS), :].astype(BF16)

        def stages(g):
            def scores():
                z_ref[g] = _dot_nt(q_ref[:, g * D:(g + 1) * D], rows(k_ref, g)) * scale

            def split(r0):
                sp = _softplus_fast(z_ref[g, r0:r0 + rc, :])
                if masked:
                    sp = jnp.where(valid(r0), sp, 0.0)
                hi, lo = _split2(sp)
                hi_ref[g, r0:r0 + rc, :] = hi
                lo_ref[g, r0:r0 + rc, :] = lo

            def suffix(s):
                sl = slice(s * SB_SUB, (s + 1) * SB_SUB)
                cs = _dot(hi_ref[g, :, sl], tri) + _dot(lo_ref[g, :, sl], tri)
                carry = carry_ref[g]
                cs_ref[g, :, sl] = cs + carry
                carry_ref[g] = carry + cs[:, 0:1]

            def weights(r0):
                w = jnp.exp(z_ref[g, r0:r0 + rc, :] - cs_ref[g, r0:r0 + rc, :])
                if masked:
                    w = jnp.where(valid(r0), w, 0.0)
                w_ref[g, r0:r0 + rc, :] = w.astype(BF16)

            def values():
                acc_ref[g] += _dot(w_ref[g], rows(v_ref, g))

            part = functools.partial
            return [[scores],
                    [part(split, r0) for r0 in range(0, tq, rc)],
                    [part(suffix, s) for s in reversed(range(tq // SB_SUB))],
                    [part(weights, r0) for r0 in range(0, tq, rc)],
                    [values]]

        def emit(*lists):
            n = max(len(lst) for lst in lists)
            for t in range(n):
                for lst in lists:
                    for f in lst[t * len(lst) // n:(t + 1) * len(lst) // n]:
                        f()

        per_head = [stages(g) for g in range(hg)]
        for t in range(5 + hg - 1):
            emit(*[per_head[g][t - g] for g in range(hg) if 0 <= t - g < 5])

    chunk(qi * tq, True)

    def body(it, _):
        chunk((qi - 1 - it) * tq, False)
        return 0

    lax.fori_loop(0, qi, body, 0)
    for g in range(hg):
        o_ref[:, g * D:(g + 1) * D] = acc_ref[g].astype(o_ref.dtype)


def sb_attention_prompt(q, k, v, B, L, tq=512, hg=4):
    assert L % tq == 0 and tq % SB_SUB == 0 and SB_HEADS % hg == 0 and tq % CHUNK_ROWS == 0
    nq = L // tq
    kv = lambda: pl.BlockSpec((L * SB_HEADS, SB_HEAD_DIM), lambda b, h, i: (b, 0),
                              pipeline_mode=pl.Buffered(1))
    return pl.pallas_call(
        functools.partial(_sb_prompt_kernel, tq=tq, hg=hg),
        grid=(B, SB_HEADS // hg, nq),
        in_specs=[pl.BlockSpec((tq, hg * SB_HEAD_DIM), lambda b, h, i: (b * nq + i, h)), kv(), kv()],
        out_specs=pl.BlockSpec((tq, hg * SB_HEAD_DIM), lambda b, h, i: (b * nq + i, h)),
        out_shape=jax.ShapeDtypeStruct((B * L, SB_HEADS * SB_HEAD_DIM), BF16),
        scratch_shapes=[pltpu.VMEM((hg, tq, SB_HEAD_DIM), F32), pltpu.VMEM((hg, tq, 1), F32),
                        pltpu.VMEM((hg, tq, tq), F32), pltpu.VMEM((hg, tq, tq), F32),
                        pltpu.VMEM((hg, tq, tq), BF16), pltpu.VMEM((hg, tq, tq), BF16),
                        pltpu.VMEM((hg, tq, tq), BF16)],
        compiler_params=_params("parallel", "parallel", "arbitrary"),
        name="sb_attention_prompt",
    )(q, k, v)


def _sb_decode_kernel(qbd_ref, kn_ref, vn_ref, pk_ref, pv_ref, o_ref, knew_ref, vnew_ref, acc_ref,
                      *, ls, npast):
    H, D = SB_HEADS, SB_HEAD_DIM
    W = H * ls
    qbd = qbd_ref[...]
    scale = D ** -0.5
    acc_ref[...] = jnp.zeros_like(acc_ref)

    def block(k_row, v_row, nb, masked, carry):
        zt = _dot(k_row, qbd) * scale
        sp = _softplus_fast(zt)
        if masked:
            j = lax.broadcasted_iota(jnp.int32, (nb, W), 0)
            t = jnp.bitwise_and(lax.broadcasted_iota(jnp.int32, (nb, W), 1), ls - 1)
            valid = j < t
            sp = jnp.where(valid, sp, 0.0)
        rs = lax.broadcasted_iota(jnp.int32, (nb, nb), 0)
        rj = lax.broadcasted_iota(jnp.int32, (nb, nb), 1)
        tri_t = jnp.where(rj >= rs, 1.0, 0.0).astype(BF16)
        hi, lo = _split2(sp)
        cs = _dot(tri_t, hi) + _dot(tri_t, lo)
        w = jnp.exp(zt - cs - carry)
        if masked:
            w = jnp.where(valid, w, 0.0)
        acc_ref[...] += _dot(w.T.astype(BF16), v_row)
        return carry + cs[0:1, :]

    def rows(ref, k0, n):
        return jnp.concatenate(
            [ref[pl.ds(k0 * H + h, n, stride=H), :] for h in range(H)], axis=1).astype(BF16)

    knew_ref[...] = jnp.zeros_like(knew_ref)
    vnew_ref[...] = jnp.zeros_like(vnew_ref)
    knew_ref[0:ls, :] = rows(kn_ref, 0, ls)
    vnew_ref[0:ls, :] = rows(vn_ref, 0, ls)
    carry = block(knew_ref[...], vnew_ref[...], LANES, True, jnp.zeros((1, W), F32))

    def body(it, carry):
        k0 = pl.multiple_of((npast - 1 - it) * SB_SUB, SB_SUB)
        return block(rows(pk_ref, k0, SB_SUB), rows(pv_ref, k0, SB_SUB), SB_SUB, False, carry)

    lax.fori_loop(0, npast, body, carry)
    for h in range(H):
        o_ref[:, h * D:(h + 1) * D] = acc_ref[h * ls:(h + 1) * ls, h * D:(h + 1) * D].astype(o_ref.dtype)


def sb_attention_decode(q, kn, vn, cache_k, cache_v, layer, B, ls):
    H, D = SB_HEADS, SB_HEAD_DIM
    depth, _, P = cache_k.shape[:3]
    W = H * ls
    assert P % SB_SUB == 0 and ls <= LANES and ls & (ls - 1) == 0 and ls % SUBLANES == 0
    qbd = jnp.einsum("bthd,hg->bhdgt", q.reshape(B, ls, H, D), jnp.eye(H, dtype=q.dtype))
    qbd = qbd.reshape(B, H * D, W)
    row = pl.BlockSpec((ls, H * D), lambda b: (b, 0))
    new = pl.BlockSpec((ls * H, D), lambda b: (b, 0))
    past = pl.BlockSpec((None, None, P * H, D), lambda b: (layer, b, 0, 0))
    return pl.pallas_call(
        functools.partial(_sb_decode_kernel, ls=ls, npast=P // SB_SUB),
        grid=(B,),
        in_specs=[pl.BlockSpec((None, H * D, W), lambda b: (b, 0, 0)), new, new, past, past],
        out_specs=row,
        out_shape=jax.ShapeDtypeStruct((B * ls, H * D), BF16),
        scratch_shapes=[pltpu.VMEM((LANES, H * D), BF16), pltpu.VMEM((LANES, H * D), BF16),
                        pltpu.VMEM((W, H * D), F32)],
        compiler_params=_params("parallel"),
        name="sb_attention_decode",
    )(qbd, kn, vn, cache_k.reshape(depth, B, P * H, D), cache_v.reshape(depth, B, P * H, D))


def _conv_taps(xp_ref, x3, halo, K, sl):
    xp_ref[:, SUBLANES - (K - 1):SUBLANES, :] = halo
    xp_ref[:, SUBLANES:SUBLANES + sl, :] = x3
    base = SUBLANES - (K - 1)
    return [xp_ref[:, base + k:base + k + sl, :] for k in range(K)]


def conv_halos(x, col0, state, B, L, tile):
    km1, C = state.shape[1:]
    nt = L // tile
    if nt == 1:
        return state[:, None]
    tails = x.reshape(B, nt, tile, x.shape[1])[:, :-1, tile - km1:, col0:col0 + C]
    return jnp.concatenate([state[:, None], tails], axis=1)


def _lru_kernel(lx_ref, lg_ref, halo_ref, h0_ref, cw_ref, cb_ref, wa_ref, ba_ref, wx_ref, bx_ref,
                lam_ref, o_ref, hl_ref, xp_ref, a_ref, u_ref, h_ref, hc_ref, *, tl, nl):
    i = pl.program_id(1)
    W = lx_ref.shape[1]

    @pl.when(i == 0)
    def _():
        hc_ref[...] = h0_ref[...]

    taps = _conv_taps(xp_ref, lx_ref[...][None], halo_ref[...][None], 4, tl)
    xc = cb_ref[...]
    for k in range(4):
        xc = xc + cw_ref[k:k + 1, :] * taps[k][0]

    bd = W // LRU_BLOCKS
    for n in range(LRU_BLOCKS):
        sl = slice(n * bd, (n + 1) * bd)
        xn = xc[:, sl]
        xb = xn.astype(BF16)
        r = jax.nn.sigmoid(_dot(xb, wa_ref[n].astype(BF16)) + ba_ref[:, sl])
        ig = jax.nn.sigmoid(_dot(xb, wx_ref[n].astype(BF16)) + bx_ref[:, sl])
        log_a = LRU_C * r * (-_softplus(-lam_ref[:, sl]))
        a = jnp.exp(log_a)
        u = jnp.sqrt(-jnp.tanh(log_a) * (a * a + 1.0)) * (ig * xn)
        a_ref[:, sl] = a
        u_ref[:, sl] = u

    def step(t, h):
        h = a_ref[pl.ds(t, 1), :] * h + u_ref[pl.ds(t, 1), :]
        h_ref[pl.ds(t, 1), :] = h
        return h

    h = lax.fori_loop(0, tl, step, hc_ref[...], unroll=8)
    hc_ref[...] = h
    o_ref[...] = (h_ref[...] * _gelu_tanh(lg_ref[...])).astype(o_ref.dtype)

    @pl.when(i == nl - 1)
    def _():
        hl_ref[...] = h


def lru_branch(src, lx_col, lg_col, state, h0, cw, cb, wa, ba, wx, bx, lam, B, L, tl=256):
    W = state.shape[2]
    tl = min(tl, L)
    nl = L // tl
    halo = conv_halos(src, lx_col * W, state, B, L, tl)
    kern = functools.partial(_lru_kernel, tl=tl, nl=nl)
    vec = lambda: pl.BlockSpec((1, W), lambda b, i: (0, 0))
    blk = lambda: pl.BlockSpec(wa.shape, lambda b, i: (0, 0, 0))
    lx, lg = src, src
    o, hl = pl.pallas_call(
        kern,
        grid=(B, nl),
        in_specs=[
            pl.BlockSpec((tl, W), lambda b, i: (b * nl + i, lx_col)),
            pl.BlockSpec((tl, W), lambda b, i: (b * nl + i, lg_col)),
            pl.BlockSpec((None, None, 3, W), lambda b, i: (b, i, 0, 0)),
            pl.BlockSpec((None, 1, W), lambda b, i: (b, 0, 0)),
            pl.BlockSpec((4, W), lambda b, i: (0, 0)),
            vec(), blk(), vec(), blk(), vec(), vec(),
        ],
        out_specs=[
            pl.BlockSpec((tl, W), lambda b, i: (b * nl + i, 0)),
            pl.BlockSpec((None, 1, W), lambda b, i: (b, 0, 0)),
        ],
        out_shape=[
            jax.ShapeDtypeStruct((B * L, W), BF16),
            jax.ShapeDtypeStruct((B, 1, W), F32),
        ],
        scratch_shapes=[
            pltpu.VMEM((1, tl + SUBLANES, W), F32),
            pltpu.VMEM((tl, W), F32),
            pltpu.VMEM((tl, W), F32),
            pltpu.VMEM((tl, W), F32),
            pltpu.VMEM((1, W), F32),
        ],
        compiler_params=_params("parallel", "arbitrary"),
        name="lru_branch",
    )(lx, lg, halo, h0.reshape(B, 1, W), cw, cb.reshape(1, W), wa, ba.reshape(1, W),
      wx, bx.reshape(1, W), lam.reshape(1, W))
    return o, hl.reshape(B, W)


def _ssd_kernel(z_ref, xbc_ref, dt_ref, halo_ref, h0_ref, cw_ref, cb_ref, dtb_ref, alog_ref,
                dexp_ref, nw_ref, o_ref, hl_ref, xp_ref, y_ref, st_ref, *, q, nc, n_valid):
    c = pl.program_id(1)
    inner = SSD_HEADS * SSD_HEAD_DIM
    gw = inner // SSD_GROUPS
    epg = SSD_HEADS // SSD_GROUPS
    n = SSD_STATE

    @pl.when(c == 0)
    def _():
        st_ref[...] = h0_ref[...]

    taps = _conv_taps(xp_ref, xbc_ref[...][None], halo_ref[...][None], 4, q)
    xc = cb_ref[...]
    for k in range(4):
        xc = xc + cw_ref[k:k + 1, :] * taps[k][0]
    xc = _silu(xc)
    xs = xc[:, :inner]
    bm = xc[:, inner:inner + SSD_GROUPS * n]
    cm = xc[:, inner + SSD_GROUPS * n:]

    rowi = lax.broadcasted_iota(jnp.int32, (q, LANES), 0)
    lanei = lax.broadcasted_iota(jnp.int32, (q, LANES), 1)
    dts = jnp.where(lanei < SSD_HEADS, _softplus(dt_ref[...] + dtb_ref[...]), 0.0)
    if n_valid < q:
        dts = jnp.where(rowi < n_valid, dts, 0.0)
    dta = dts * (-jnp.exp(alog_ref[...]))

    ti = lax.broadcasted_iota(jnp.int32, (q, q), 0)
    si = lax.broadcasted_iota(jnp.int32, (q, q), 1)
    causal = si <= ti
    lower = jnp.where(causal, 1.0, 0.0).astype(BF16)
    cum = sum(_dot(lower, p) for p in _split3(dta))
    cum_t = cum.T
    cum_last = cum[q - 1:q, :]
    exp_cum = jnp.exp(cum)
    decay_end = jnp.exp(cum_last - cum)

    eh = lax.broadcasted_iota(jnp.int32, (LANES, inner), 0)
    ec = lax.broadcasted_iota(jnp.int32, (LANES, inner), 1)
    expand = jnp.where(ec // SSD_HEAD_DIM == eh, 1.0, 0.0).astype(BF16)

    def expand_heads(a):
        return sum(_dot(p, expand) for p in _split3(a))

    dts_e = expand_heads(dts)
    exp_cum_e = expand_heads(exp_cum)
    decay_end_e = expand_heads(decay_end)
    chunk_decay_e = exp_cum_e[q - 1:q, :]

    xdt = xs * dts_e
    xdt_b = xdt.astype(BF16)
    xend_b = (decay_end_e * xdt).astype(BF16)

    for g in range(SSD_GROUPS):
        bg = bm[:, g * n:(g + 1) * n]
        cg_b = cm[:, g * n:(g + 1) * n].astype(BF16)
        gs = slice(g * gw, (g + 1) * gw)
        cb = _dot_nt(cg_b, bg.astype(BF16))
        st = st_ref[g]
        y_inter = _dot(cg_b, st.astype(BF16)) * exp_cum_e[:, gs]
        for e in range(epg):
            h = g * epg + e
            hs = slice(h * SSD_HEAD_DIM, (h + 1) * SSD_HEAD_DIM)
            seg = cum[:, h:h + 1] - cum_t[h:h + 1, :]
            decay = jnp.where(causal, jnp.exp(seg), 0.0)
            wts = (cb * decay).astype(BF16)
            y_ref[:, hs] = _dot(wts, xdt_b[:, hs]) + y_inter[:, e * SSD_HEAD_DIM:(e + 1) * SSD_HEAD_DIM]
        bg_t = bg.T.astype(BF16)
        st_ref[g] = chunk_decay_e[:, gs] * st + _dot(bg_t, xend_b[:, gs])

    y = y_ref[...] + dexp_ref[...] * xs
    yg = y * _silu(z_ref[...])
    for g in range(SSD_GROUPS):
        gs = slice(g * gw, (g + 1) * gw)
        part = yg[:, gs]
        ms = jnp.mean(part * part, axis=-1, keepdims=True)
        o_ref[:, gs] = (part * lax.rsqrt(ms + NORM_EPS) * nw_ref[:, gs]).astype(o_ref.dtype)

    @pl.when(c == nc - 1)
    def _():
        hl_ref[...] = st_ref[...]


def _pad_rows(x, B, L, Lp):
    if Lp == L:
        return x
    C = x.shape[1]
    return jnp.pad(x.reshape(B, L, C), ((0, 0), (0, Lp - L), (0, 0))).reshape(B * Lp, C)


def ssd_branch(src, z_col, xbc_col, dt_col, state, h0, cw, cb, dt_bias, a_log, d, norm_w, B, L, q=128):
    inner = norm_w.shape[0]
    cd = state.shape[2]
    nh = SSD_HEADS
    Lp = max(L, q) if L % q else L
    n_valid = min(L, q)
    nc = Lp // q
    gw = inner // SSD_GROUPS
    epg = SSD_HEADS // SSD_GROUPS
    halo = conv_halos(src, xbc_col, state, B, L, q) if L >= q else state[:, None]
    if Lp != L:
        zp = _pad_rows(src[:, z_col:z_col + inner], B, L, Lp)
        xbcp = _pad_rows(src[:, xbc_col:xbc_col + cd], B, L, Lp)
        dtp = _pad_rows(src[:, dt_col:dt_col + LANES], B, L, Lp)
        z_col = xbc_col = dt_col = 0
    else:
        zp = xbcp = dtp = src
    z_blk, xbc_blk, dt_block = z_col // inner, xbc_col // cd, dt_col // LANES
    h0t = h0.reshape(B, SSD_GROUPS, epg, SSD_HEAD_DIM, SSD_STATE).transpose(0, 1, 4, 2, 3)
    h0t = h0t.reshape(B, SSD_GROUPS, SSD_STATE, gw)
    pad1 = lambda a: jnp.pad(a, (0, LANES - nh)).reshape(1, LANES)
    kern = functools.partial(_ssd_kernel, q=q, nc=nc, n_valid=n_valid)
    row = lambda w, blk=0: pl.BlockSpec((q, w), lambda b, c: (b * nc + c, blk))
    vec = lambda w: pl.BlockSpec((1, w), lambda b, c: (0, 0))
    o, hl = pl.pallas_call(
        kern,
        grid=(B, nc),
        in_specs=[
            row(inner, z_blk), row(cd, xbc_blk), row(LANES, dt_block),
            pl.BlockSpec((None, None, 3, cd), lambda b, c: (b, c, 0, 0)),
            pl.BlockSpec((None, SSD_GROUPS, SSD_STATE, gw), lambda b, c: (b, 0, 0, 0)),
            pl.BlockSpec((4, cd), lambda b, c: (0, 0)),
            vec(cd), vec(LANES), vec(LANES), vec(inner), vec(inner),
        ],
        out_specs=[
            row(inner),
            pl.BlockSpec((None, SSD_GROUPS, SSD_STATE, gw), lambda b, c: (b, 0, 0, 0)),
        ],
        out_shape=[
            jax.ShapeDtypeStruct((B * Lp, inner), BF16),
            jax.ShapeDtypeStruct((B, SSD_GROUPS, SSD_STATE, gw), F32),
        ],
        scratch_shapes=[
            pltpu.VMEM((1, q + SUBLANES, cd), F32),
            pltpu.VMEM((q, inner), F32),
            pltpu.VMEM((SSD_GROUPS, SSD_STATE, gw), F32),
        ],
        compiler_params=_params("parallel", "arbitrary"),
        name="ssd_branch",
    )(zp, xbcp, dtp, halo, h0t, cw, cb.reshape(1, cd), pad1(dt_bias), pad1(a_log),
      jnp.repeat(d, SSD_HEAD_DIM).reshape(1, inner), norm_w.reshape(1, inner))
    if Lp != L:
        o = o.reshape(B, Lp, inner)[:, :L].reshape(B * L, inner)
    hl = hl.reshape(B, SSD_GROUPS, SSD_STATE, epg, SSD_HEAD_DIM).transpose(0, 1, 3, 4, 2)
    return o, hl.reshape(B, SSD_HEADS, SSD_HEAD_DIM, SSD_STATE)


def _merge_kernel(x_ref, g_ref, a_ref, b_ref, c_ref, wg0_ref, wg1_ref, wg2_ref, wb_ref, o_ref, xn_ref):
    @pl.when(pl.program_id(1) == 0)
    def _():
        x = x_ref[...]
        ms = jnp.mean(x * x, axis=-1, keepdims=True)
        xn_ref[...] = (x * lax.rsqrt(ms + NORM_EPS) * g_ref[...]).astype(BF16)

    xn = xn_ref[...]
    acc = None
    for n, (br, wg) in enumerate(((a_ref, wg0_ref), (b_ref, wg1_ref), (c_ref, wg2_ref))):
        gate = _dot_nt(xn, wg[...].astype(BF16))
        t = _dot(br[...], wb_ref[n].astype(BF16)) * jax.nn.sigmoid(gate)
        acc = t if acc is None else acc + t
    o_ref[...] = acc.astype(o_ref.dtype)


def gated_merge(x, g, oa, ob, oc, w_gate_t, gate_row0, wb, layer, tm=1024, tn=256):
    T, K = oa.shape
    D = x.shape[1]
    N = wb.shape[-1]
    tm = min(tm, T)
    nj = N // tn
    assert gate_row0 % SUBLANES == 0
    br = lambda: pl.BlockSpec((tm, K), lambda i, j: (i, 0))
    gate = lambda n: pl.BlockSpec(
        (None, pl.Element(tn), pl.Element(D)),
        lambda i, j: (layer, pl.multiple_of(gate_row0 + (n * nj + j) * tn, SUBLANES), 0))
    return pl.pallas_call(
        _merge_kernel,
        grid=(T // tm, nj),
        in_specs=[pl.BlockSpec((tm, D), lambda i, j: (i, 0), pipeline_mode=pl.Buffered(1)),
                  pl.BlockSpec((1, D), lambda i, j: (0, 0)),
                  br(), br(), br(), gate(0), gate(1), gate(2),
                  pl.BlockSpec((None, 3, K, tn), lambda i, j: (layer, 0, 0, j))],
        out_specs=pl.BlockSpec((tm, tn), lambda i, j: (i, j)),
        out_shape=jax.ShapeDtypeStruct((T, N), BF16),
        scratch_shapes=[pltpu.VMEM((tm, D), BF16)],
        compiler_params=_params("parallel", "arbitrary"),
        name="gated_merge",
    )(x, g.reshape(1, D), oa, ob, oc, w_gate_t, w_gate_t, w_gate_t, wb)


def _xattn_kernel(q_ref, mk_ref, mv_ref, o_ref):
    hd = q_ref.shape[1] // XA_HEADS
    scale = hd ** -0.5
    for h in range(XA_HEADS):
        sl = slice(h * hd, (h + 1) * hd)
        kh, vh = (mk_ref[:, sl], mv_ref[:, sl]) if len(mk_ref.shape) == 2 else (mk_ref[:, h, :], mv_ref[:, h, :])
        s = _dot_nt(q_ref[:, sl].astype(BF16), kh.astype(BF16)) * scale
        m = jnp.max(s, axis=-1, keepdims=True)
        p = jnp.exp(s - m)
        p = p / jnp.sum(p, axis=-1, keepdims=True)
        o_ref[:, sl] = _dot(p.astype(BF16), vh.astype(BF16)).astype(o_ref.dtype)


def cross_attention(q, mk, mv, layer, B, L, tq=512):
    D = q.shape[1]
    tq = min(tq, L)
    nq = L // tq
    if layer is None:
        mem = lambda: pl.BlockSpec((None,) + mk.shape[1:], lambda b, i: (b, 0, 0))
    else:
        mem = lambda: pl.BlockSpec((None, None) + mk.shape[2:], lambda b, i: (layer, b, 0, 0, 0))
    return pl.pallas_call(
        _xattn_kernel,
        grid=(B, nq),
        in_specs=[pl.BlockSpec((tq, D), lambda b, i: (b * nq + i, 0)), mem(), mem()],
        out_specs=pl.BlockSpec((tq, D), lambda b, i: (b * nq + i, 0)),
        out_shape=jax.ShapeDtypeStruct((B * L, D), BF16),
        compiler_params=_params("parallel", "arbitrary"),
        name="cross_attention",
    )(q, mk, mv)


def _mem_kv_kernel(x_ref, g_ref, wk_ref, wv_ref, k_ref, v_ref, k2_ref, v2_ref, xn_ref):
    j = pl.program_id(1)

    @pl.when(j == 0)
    def _():
        x = x_ref[...]
        ms = jnp.mean(x * x, axis=-1, keepdims=True)
        xn_ref[...] = (x * lax.rsqrt(ms + NORM_EPS) * g_ref[...]).astype(BF16)

    xn = xn_ref[...]
    k = _dot(xn, wk_ref[...].astype(BF16))
    v = _dot(xn, wv_ref[...].astype(BF16))
    k2_ref[...] = k.astype(k2_ref.dtype)
    v2_ref[...] = v.astype(v2_ref.dtype)
    for h in range(XA_HEADS):
        @pl.when(j == h)
        def _(h=h):
            k_ref[:, h, :] = k
            v_ref[:, h, :] = v


def mem_kv(mem, g, wk, wv, layer, B, M):
    D = mem.shape[1]
    hd = D // XA_HEADS
    w = lambda: pl.BlockSpec((None, D, hd), lambda b, j: (layer, 0, j))
    o4 = lambda: pl.BlockSpec((None, M, XA_HEADS, hd), lambda b, j: (b, 0, 0, 0))
    o2 = lambda: pl.BlockSpec((None, M, hd), lambda b, j: (b, 0, j))
    return pl.pallas_call(
        _mem_kv_kernel,
        grid=(B, XA_HEADS),
        in_specs=[pl.BlockSpec((M, D), lambda b, j: (b, 0)), pl.BlockSpec((1, D), lambda b, j: (0, 0)),
                  w(), w()],
        out_specs=[o4(), o4(), o2(), o2()],
        out_shape=[jax.ShapeDtypeStruct((B, M, XA_HEADS, hd), F32)] * 2
        + [jax.ShapeDtypeStruct((B, M, D), BF16)] * 2,
        scratch_shapes=[pltpu.VMEM((M, D), BF16)],
        compiler_params=_params("parallel", "arbitrary"),
        name="mem_kv",
    )(mem, g.reshape(1, D), wk, wv)


def _ffn_up_kernel(x_ref, g_ref, wg_ref, wv_ref, sg_ref, sv_ref, cwg_ref, cwv_ref, cbg_ref, cbv_ref,
                   act_ref, tg_ref, tv_ref, xn_ref, xp_ref, carry_ref, *, nseg, sl, tpb):
    i = pl.program_id(0)
    f = pl.program_id(1)
    tm, tf = act_ref.shape

    @pl.when(f == 0)
    def _():
        x = x_ref[...]
        ms = jnp.mean(x * x, axis=-1, keepdims=True)
        xn_ref[...] = (x * lax.rsqrt(ms + NORM_EPS) * g_ref[...]).astype(BF16)

    if tpb > 1:
        @pl.when((i == 0) & (f == 0))
        def _():
            carry_ref[...] = jnp.zeros_like(carry_ref)

    xn = xn_ref[...]
    base = SUBLANES - 2
    starts = i % tpb == 0
    ts = xp_ref.shape[-1]

    def half(s, c, w_ref, st_ref, cw_ref, cb_ref, tail_ref):
        cols = slice(s * ts, (s + 1) * ts)
        xp = xp_ref.at[2 * s + c]
        u3 = _dot(xn, w_ref[:, cols].astype(BF16)).reshape(nseg, sl, ts)
        halo = st_ref[:, :, cols]
        if tpb > 1:
            halo = jnp.where(starts, halo, carry_ref[c, f, :, :, cols])
        xp[:, base:SUBLANES, :] = halo
        xp[:, SUBLANES:SUBLANES + sl, :] = u3
        y = cb_ref[:, cols]
        for k in range(3):
            y = y + cw_ref[k:k + 1, cols] * xp[:, base + k:base + k + sl, :].reshape(tm, ts)
        tail = xp[:, SUBLANES + sl - 2:SUBLANES + sl, :]
        if tpb > 1:
            carry_ref[c, f, :, :, cols] = tail
        tail_ref[:, :, cols] = tail
        return y

    for s in range(tf // ts):
        gate = half(s, 0, wg_ref, sg_ref, cwg_ref, cbg_ref, tg_ref)
        val = half(s, 1, wv_ref, sv_ref, cwv_ref, cbv_ref, tv_ref)
        act_ref[:, s * ts:(s + 1) * ts] = (_silu(gate) * val).astype(act_ref.dtype)


def ffn_up(x, g, w_up, layer, state, cw, cb, B, L, tm=1024, tf=512):
    T, D = x.shape
    F2 = w_up.shape[-1]
    F = F2 // 2
    if L >= tm:
        nseg, sl, tpb = 1, tm, L // tm
    else:
        nseg, sl, tpb = min(tm // L, B), L, 1
        tm = nseg * sl
    nt, nf = T // tm, F // tf
    ts = min(tf, MXU_WIDTH)
    st = state.reshape(B // nseg, nseg, 2, F2)
    cb2 = cb.reshape(1, F2)
    kern = functools.partial(_ffn_up_kernel, nseg=nseg, sl=sl, tpb=tpb)
    wspec = lambda off: pl.BlockSpec((None, D, tf), lambda i, f: (layer, 0, off + f))
    sspec = lambda off: pl.BlockSpec((None, nseg, 2, tf), lambda i, f: (i // tpb, 0, 0, off + f))
    cwspec = lambda off: pl.BlockSpec((3, tf), lambda i, f: (0, off + f))
    cbspec = lambda off: pl.BlockSpec((1, tf), lambda i, f: (0, off + f))
    tspec = lambda: pl.BlockSpec((None, nseg, 2, tf), lambda i, f: (i, 0, 0, f))
    act, tg, tv = pl.pallas_call(
        kern,
        grid=(nt, nf),
        in_specs=[
            pl.BlockSpec((tm, D), lambda i, f: (i, 0), pipeline_mode=pl.Buffered(1)),
            pl.BlockSpec((1, D), lambda i, f: (0, 0)),
            wspec(0), wspec(nf), sspec(0), sspec(nf), cwspec(0), cwspec(nf), cbspec(0), cbspec(nf),
        ],
        out_specs=[pl.BlockSpec((tm, tf), lambda i, f: (i, f)), tspec(), tspec()],
        out_shape=[
            jax.ShapeDtypeStruct((T, F), BF16),
            jax.ShapeDtypeStruct((nt, nseg, 2, F), F32),
            jax.ShapeDtypeStruct((nt, nseg, 2, F), F32),
        ],
        scratch_shapes=[
            pltpu.VMEM((tm, D), BF16),
            pltpu.VMEM((2 * (tf // ts), nseg, sl + SUBLANES, ts), F32),
            pltpu.VMEM((2, nf, nseg, 2, tf), F32),
        ],
        compiler_params=_params("arbitrary", "arbitrary"),
        name="ffn_up",
    )(x, g.reshape(1, D), w_up, w_up, st, st, cw, cw, cb2, cb2)
    last = lambda t: t.reshape(B // nseg, tpb, nseg, 2, F)[:, -1].reshape(B, 2, F)
    return act, jnp.concatenate([last(tg), last(tv)], axis=-1)


def _last_rows(x, col0, state, B, L):
    km1, C = state.shape[1:]
    x3 = x.reshape(B, L, x.shape[1])
    if L >= km1:
        return x3[:, L - km1:, col0:col0 + C]
    return jnp.concatenate([state, x3[:, :, col0:col0 + C]], axis=1)[:, -km1:]


def trunk_layer(x, B, L, mk, mv, mem_layer, past_k, past_v, lru_buf, lru_h, ssd_buf, ssd_h, ffn_buf, w, l):
    mix = SB_HEADS * SB_HEAD_DIM
    inner = SSD_HEADS * SSD_HEAD_DIM
    cd = inner + 2 * SSD_GROUPS * SSD_STATE
    D = x.shape[1]
    tn = 512
    q, k, v, rest = in_proj(x, w["norm_mix"][l], w["w_in_t"], l, (mix, mix, mix, 2 * mix + inner + cd + tn),
                            (BF16, F32, F32, F32), (False, True, True, False), SB_HEAD_DIM, tn=tn)
    c_z, c_xbc, c_dt = 2 * mix, 2 * mix + inner, 2 * mix + inner + cd

    if past_k is None:
        o_a = sb_attention_prompt(q, k, v, B, L)
    else:
        o_a = sb_attention_decode(q, k, v, past_k, past_v, l, B, L)
    o_b, lru_h_new = lru_branch(rest, 0, 1, lru_buf, lru_h, w["lru_conv_w"][l], w["lru_conv_b"][l],
                                w["lru_wa"][l], w["lru_ba"][l], w["lru_wx"][l], w["lru_bx"][l],
                                w["lru_lam"][l], B, L)
    o_c, ssd_h_new = ssd_branch(rest, c_z, c_xbc, c_dt, ssd_buf, ssd_h, w["ssd_conv_w"][l],
                                w["ssd_conv_b"][l], w["ssd_dt_bias"][l], w["ssd_a_log"][l], w["ssd_d"][l],
                                w["ssd_norm"][l], B, L)
    merged = gated_merge(x, w["norm_mix"][l], o_a, o_b, o_c, w["w_in_t"], 5 * mix + inner + cd + SSD_HEADS,
                         w["w_branch"], l)
    h = matmul_res(merged, w["w_out"], l, x, tm=2048)
    qx = norm_matmul(h, w["norm_xattn"][l], w["w_xq"], l, out_dtype=BF16, tn=1024)
    o_x = cross_attention(qx, mk, mv, mem_layer, B, L)
    h = matmul_res(o_x, w["w_xo"], l, h, tm=2048)
    act, ffn_new = ffn_up(h, w["norm_ffn"][l], w["w_up"], l, ffn_buf, w["ffn_conv_w"][l],
                          w["ffn_conv_b"][l], B, L)
    out = matmul_res(act, w["w_down"], l, h, tn=256)
    states = (k.reshape(B, L, SB_HEADS, SB_HEAD_DIM), v.reshape(B, L, SB_HEADS, SB_HEAD_DIM),
              _last_rows(rest, 0, lru_buf, B, L), lru_h_new, _last_rows(rest, c_xbc, ssd_buf, B, L),
              ssd_h_new, ffn_new)
    return out, states


def kernel(x_prompt, x_sample, mem_prompt, cache_sb_k, cache_sb_v, cache_mem_k, cache_mem_v, state_lru_conv, state_lru_h, state_ssd_conv, state_ssd, state_ffn_conv, norm_mix, w_in, lru_conv_w, lru_conv_b, lru_wa, lru_ba, lru_wx, lru_bx, lru_lam, ssd_conv_w, ssd_conv_b, ssd_dt_bias, ssd_a_log, ssd_d, ssd_norm, w_branch, w_out, norm_xattn, norm_mem, w_xq, w_xk, w_xv, w_xo, norm_ffn, w_up, ffn_conv_w, ffn_conv_b, w_down, norm_final):
    depth = w_in.shape[0]
    Bp, Lp, D = x_prompt.shape
    Bs, Ls, _ = x_sample.shape
    M = mem_prompt.shape[1]
    mix = SB_HEADS * SB_HEAD_DIM
    inner = SSD_HEADS * SSD_HEAD_DIM
    cd = inner + 2 * SSD_GROUPS * SSD_STATE
    f2 = w_up.shape[2]
    xa_hd = D // XA_HEADS

    yp = x_prompt.reshape(Bp * Lp, D)
    ys = x_sample.reshape(Bs * Ls, D)
    mem = mem_prompt.reshape(Bp * M, D)
    st_p, st_s, mk_list, mv_list = [], [], [], []
    w_in_t = jnp.swapaxes(w_in, 1, 2)
    w = dict(w_in_t=w_in_t, norm_mix=norm_mix, lru_conv_w=lru_conv_w, lru_conv_b=lru_conv_b,
             lru_wa=lru_wa, lru_ba=lru_ba, lru_wx=lru_wx, lru_bx=lru_bx, lru_lam=lru_lam,
             ssd_conv_w=ssd_conv_w, ssd_conv_b=ssd_conv_b, ssd_dt_bias=ssd_dt_bias, ssd_a_log=ssd_a_log,
             ssd_d=ssd_d, ssd_norm=ssd_norm, w_branch=w_branch, w_out=w_out, norm_xattn=norm_xattn,
             w_xq=w_xq, w_xo=w_xo, norm_ffn=norm_ffn, w_up=w_up, ffn_conv_w=ffn_conv_w,
             ffn_conv_b=ffn_conv_b, w_down=w_down)
    for l in range(depth):
        mk_p, mv_p, mk2, mv2 = mem_kv(mem, norm_mem[l], w_xk, w_xv, l, Bp, M)
        zeros = lambda *s: jnp.zeros(s, F32)
        yp, sp = trunk_layer(yp, Bp, Lp, mk2, mv2, None, None, None, zeros(Bp, 3, mix),
                             zeros(Bp, mix), zeros(Bp, 3, cd), zeros(Bp, SSD_HEADS, SSD_HEAD_DIM, SSD_STATE),
                             zeros(Bp, 2, f2), w, l)
        ys, ss = trunk_layer(ys, Bs, Ls, cache_mem_k, cache_mem_v, l,
                             cache_sb_k, cache_sb_v, state_lru_conv[l], state_lru_h[l],
                             state_ssd_conv[l], state_ssd[l], state_ffn_conv[l], w, l)
        st_p.append(sp)
        st_s.append(ss)
        mk_list.append(mk_p)
        mv_list.append(mv_p)
    y_prompt = rmsnorm(yp, norm_final).reshape(Bp, Lp, D)
    y_sample = rmsnorm(ys, norm_final).reshape(Bs, Ls, D)

    def stk(lst, i):
        return jnp.stack([s[i] for s in lst], axis=0)

    return (y_prompt, y_sample,
            stk(st_p, 0), stk(st_p, 1), stk(st_p, 2), stk(st_p, 3), stk(st_p, 4), stk(st_p, 5), stk(st_p, 6),
            jnp.stack(mk_list, axis=0), jnp.stack(mv_list, axis=0),
            stk(st_s, 0), stk(st_s, 1), stk(st_s, 2), stk(st_s, 3), stk(st_s, 4), stk(st_s, 5), stk(st_s, 6))
```
